```python
import math
import jax, jax.numpy as jnp
from jax import lax
import numpy as np

D_MODEL = 1024
BATCH = 8
SEQ = 4096
DEPTH = 2

N_MIXERS = 2
N_FNET_LAYERS = (DEPTH + N_MIXERS - 1) // N_MIXERS
N_SSD_LAYERS = DEPTH // N_MIXERS
LN_EPS = 1e-5
DEEPNORM_ALPHA = (2.0 * DEPTH) ** 0.25
DEEPNORM_BETA = (8.0 * DEPTH) ** -0.25

FN_WIDTH = D_MODEL
FN_GROUPS = 8
FN_GROUP_DIM = FN_WIDTH // FN_GROUPS

SSM_EXPAND = 2
D_INNER = SSM_EXPAND * D_MODEL
HEAD_DIM = 64
N_SSM_HEADS = D_INNER // HEAD_DIM
SSM_GROUPS = 8
HEADS_PER_GROUP = N_SSM_HEADS // SSM_GROUPS
D_STATE = 128
SSM_CONV = 5
CHUNK = 128
CONV_DIM = D_INNER + 2 * SSM_GROUPS * D_STATE
D_IN_PROJ = D_INNER + CONV_DIM + 2 * N_SSM_HEADS

D_FF = 2816
FF_CONV = 3

kernel_name = "bidir_fnet_ssd_hybrid_deepnorm"


def layer_norm(x, g, b):
    xf = x.astype(jnp.float32)
    mu = jnp.mean(xf, axis=-1, keepdims=True)
    xc = xf - mu
    var = jnp.mean(xc * xc, axis=-1, keepdims=True)
    return (xc * lax.rsqrt(var + LN_EPS) * g + b).astype(x.dtype)


def depthwise_conv(x, w, b):
    k = w.shape[0]
    y = lax.conv_general_dilated(
        x, w[:, None, :], window_strides=(1,), padding=[(k // 2, k // 2)],
        dimension_numbers=("NWC", "WIO", "NWC"), feature_group_count=x.shape[-1])
    return y + b


def fourier_mixer(x, w_in, b_in, w_out, b_out):
    bsz, s, _ = x.shape
    h = (x @ w_in + b_in).reshape(bsz, s, FN_GROUPS, FN_GROUP_DIM)
    f = jnp.fft.fft2(h.astype(jnp.float32), axes=(1, 3), norm="ortho").real
    return f.reshape(bsz, s, FN_WIDTH).astype(x.dtype) @ w_out + b_out


def ssd_scan(x, dt, a, bm, cm):
    b, s, g, r, p = x.shape
    n = bm.shape[-1]
    nc = s // CHUNK
    xc = x.reshape(b, nc, CHUNK, g, r, p)
    dtc = dt.reshape(b, nc, CHUNK, g, r)
    bc = bm.reshape(b, nc, CHUNK, g, n)
    cc = cm.reshape(b, nc, CHUNK, g, n)
    a_cum = jnp.cumsum(dtc * a, axis=2)
    lower = jnp.tril(jnp.ones((CHUNK, CHUNK), dtype=bool))
    seg = a_cum[:, :, :, None] - a_cum[:, :, None, :]
    decay = jnp.exp(jnp.where(lower[:, :, None, None], seg, -jnp.inf))
    cb = jnp.einsum("bclgn,bcsgn->bclsg", cc, bc)
    y_diag = jnp.einsum("bclsgr,bcsgrp->bclgrp",
                        cb[..., None] * decay * dtc[:, :, None], xc)
    decay_to_end = jnp.exp(a_cum[:, :, -1:] - a_cum)
    states = jnp.einsum("bcsgn,bcsgrp->bcgrpn", bc,
                        xc * (decay_to_end * dtc)[..., None])
    chunk_decay = jnp.exp(a_cum[:, :, -1])

    def _step(h, inp):
        st, dec = inp
        return h * dec[..., None, None] + st, h

    _, prev = lax.scan(_step, jnp.zeros_like(states[:, 0]),
                       (jnp.moveaxis(states, 1, 0), jnp.moveaxis(chunk_decay, 1, 0)))
    prev = jnp.moveaxis(prev, 0, 1)
    y_off = jnp.einsum("bclgn,bcgrpn->bclgrp", cc, prev) * jnp.exp(a_cum)[..., None]
    return (y_diag + y_off).reshape(b, s, g, r, p)


def ssd_mixer(x, w_in, conv_w, conv_b, a_log_f, a_log_b, dt_bias_f, dt_bias_b,
              d_skip, norm_g, w_out):
    bsz, s, _ = x.shape
    f32 = jnp.float32
    proj = x @ w_in
    z, xbc, dt_raw = jnp.split(proj, [D_INNER, D_INNER + CONV_DIM], axis=-1)
    xbc = jax.nn.silu(depthwise_conv(xbc, conv_w, conv_b))
    xs, bm, cm = jnp.split(xbc, [D_INNER, D_INNER + SSM_GROUPS * D_STATE], axis=-1)
    xs = xs.astype(f32).reshape(bsz, s, SSM_GROUPS, HEADS_PER_GROUP, HEAD_DIM)
    bm = bm.astype(f32).reshape(bsz, s, SSM_GROUPS, D_STATE)
    cm = cm.astype(f32).reshape(bsz, s, SSM_GROUPS, D_STATE)
    dt_f, dt_b = jnp.split(dt_raw.astype(f32), 2, axis=-1)
    dt_f = jax.nn.softplus(dt_f + dt_bias_f.astype(f32)).reshape(bsz, s, SSM_GROUPS, HEADS_PER_GROUP)
    dt_b = jax.nn.softplus(dt_b + dt_bias_b.astype(f32)).reshape(bsz, s, SSM_GROUPS, HEADS_PER_GROUP)
    a_f = -jnp.exp(a_log_f.astype(f32)).reshape(SSM_GROUPS, HEADS_PER_GROUP)
    a_b = -jnp.exp(a_log_b.astype(f32)).reshape(SSM_GROUPS, HEADS_PER_GROUP)
    y_f = ssd_scan(xs, dt_f, a_f, bm, cm)
    y_b = jnp.flip(ssd_scan(jnp.flip(xs, 1), jnp.flip(dt_b, 1), a_b,
                            jnp.flip(bm, 1), jnp.flip(cm, 1)), 1)
    y = y_f + y_b + d_skip.astype(f32).reshape(SSM_GROUPS, HEADS_PER_GROUP, 1) * xs
    y = y.reshape(bsz, s, D_INNER) * jax.nn.silu(z.astype(f32))
    yg = y.reshape(bsz, s, SSM_GROUPS, D_INNER // SSM_GROUPS)
    yg = yg * lax.rsqrt(jnp.mean(yg * yg, axis=-1, keepdims=True) + LN_EPS)
    y = yg.reshape(bsz, s, D_INNER) * norm_g
    return y.astype(x.dtype) @ w_out


def conv_ffn(x, w_up, conv_w, conv_b, w_down):
    h = depthwise_conv(x @ w_up, conv_w, conv_b)
    g, v = jnp.split(h, 2, axis=-1)
    return (jax.nn.silu(g) * v) @ w_down


def setup_inputs(seed: int = 0) -> dict:
    key = jax.random.key(seed)
    ks = jax.random.split(key, 32)
    f32 = jnp.float32

    def nrm(k, shape, scale):
        return scale * jax.random.normal(k, shape, f32)

    d = D_MODEL
    dt = jnp.exp(jax.random.uniform(ks[11], (N_SSD_LAYERS, N_SSM_HEADS), f32,
                                    math.log(1e-3), math.log(1e-1)))
    dt2 = jnp.exp(jax.random.uniform(ks[12], (N_SSD_LAYERS, N_SSM_HEADS), f32,
                                     math.log(1e-3), math.log(1e-1)))
    return {
        "x": nrm(ks[0], (BATCH, SEQ, d), 1.0),
        "emb_ln_g": 1.0 + nrm(ks[1], (d,), 0.02),
        "emb_ln_b": nrm(ks[2], (d,), 0.02),
        "fn_w_in": nrm(ks[3], (N_FNET_LAYERS, d, FN_WIDTH), d ** -0.5),
        "fn_b_in": nrm(ks[4], (N_FNET_LAYERS, FN_WIDTH), 0.02),
        "fn_w_out": nrm(ks[5], (N_FNET_LAYERS, FN_WIDTH, d), FN_WIDTH ** -0.5 * DEEPNORM_BETA),
        "fn_b_out": nrm(ks[6], (N_FNET_LAYERS, d), 0.02),
        "ssd_w_in": nrm(ks[7], (N_SSD_LAYERS, d, D_IN_PROJ), d ** -0.5),
        "ssd_conv_w": nrm(ks[8], (N_SSD_LAYERS, SSM_CONV, CONV_DIM), SSM_CONV ** -0.5),
        "ssd_conv_b": nrm(ks[9], (N_SSD_LAYERS, CONV_DIM), 0.02),
        "ssd_a_log_fwd": jnp.log(jax.random.uniform(ks[10], (N_SSD_LAYERS, N_SSM_HEADS), f32, 1.0, 16.0)),
        "ssd_a_log_bwd": jnp.log(jax.random.uniform(ks[13], (N_SSD_LAYERS, N_SSM_HEADS), f32, 1.0, 16.0)),
        "ssd_dt_bias_fwd": dt + jnp.log(-jnp.expm1(-dt)),
        "ssd_dt_bias_bwd": dt2 + jnp.log(-jnp.expm1(-dt2)),
        "ssd_d": 1.0 + nrm(ks[14], (N_SSD_LAYERS, N_SSM_HEADS), 0.1),
        "ssd_norm_g": 1.0 + nrm(ks[15], (N_SSD_LAYERS, D_INNER), 0.02),
        "ssd_w_out": nrm(ks[16], (N_SSD_LAYERS, D_INNER, d), D_INNER ** -0.5 * DEEPNORM_BETA),
        "ln_tok_g": 1.0 + nrm(ks[17], (DEPTH, d), 0.02),
        "ln_tok_b": nrm(ks[18], (DEPTH, d), 0.02),
        "ff_w_up": nrm(ks[19], (DEPTH, d, 2 * D_FF), d ** -0.5),
        "ff_conv_w": nrm(ks[20], (DEPTH, FF_CONV, 2 * D_FF), FF_CONV ** -0.5),
        "ff_conv_b": nrm(ks[21], (DEPTH, 2 * D_FF), 0.02),
        "ff_w_down": nrm(ks[22], (DEPTH, D_FF, d), D_FF ** -0.5 * DEEPNORM_BETA),
        "ln_ffn_g": 1.0 + nrm(ks[23], (DEPTH, d), 0.02),
        "ln_ffn_b": nrm(ks[24], (DEPTH, d), 0.02),
    }


def reference(x, emb_ln_g, emb_ln_b, fn_w_in, fn_b_in, fn_w_out, fn_b_out,
              ssd_w_in, ssd_conv_w, ssd_conv_b, ssd_a_log_fwd, ssd_a_log_bwd,
              ssd_dt_bias_fwd, ssd_dt_bias_bwd, ssd_d, ssd_norm_g, ssd_w_out,
              ln_tok_g, ln_tok_b, ff_w_up, ff_conv_w, ff_conv_b, ff_w_down,
              ln_ffn_g, ln_ffn_b):
    h = layer_norm(x, emb_ln_g, emb_ln_b)
    for i in range(DEPTH):
        j = i // N_MIXERS
        if i % N_MIXERS == 0:
            mix = fourier_mixer(h, fn_w_in[j], fn_b_in[j], fn_w_out[j], fn_b_out[j])
        else:
            mix = ssd_mixer(h, ssd_w_in[j], ssd_conv_w[j], ssd_conv_b[j],
                            ssd_a_log_fwd[j], ssd_a_log_bwd[j],
                            ssd_dt_bias_fwd[j], ssd_dt_bias_bwd[j],
                            ssd_d[j], ssd_norm_g[j], ssd_w_out[j])
        h = layer_norm(DEEPNORM_ALPHA * h + mix, ln_tok_g[i], ln_tok_b[i])
        ffn = conv_ffn(h, ff_w_up[i], ff_conv_w[i], ff_conv_b[i], ff_w_down[i])
        h = layer_norm(DEEPNORM_ALPHA * h + ffn, ln_ffn_g[i], ln_ffn_b[i])
    return h
```

```python
import functools
import math

import numpy as np
import jax
import jax.numpy as jnp
from jax import lax
from jax.experimental import pallas as pl
from jax.experimental.pallas import tpu as pltpu

F32 = jnp.float32
BF16 = jnp.bfloat16

LN_EPS = 1e-5
DEPTH = 2
N_MIXERS = 2
DEEPNORM_ALPHA = (2.0 * DEPTH) ** 0.25
FN_GROUPS = 8
HEAD_DIM = 64
D_STATE = 128
SSM_GROUPS = 8
SSM_EXPAND = 2
CHUNK = 128

V7X_LANES = 128
V7X_BF16_SUBLANES = 16
V7X_VMEM_LIMIT_BYTES = 56 * 1024 * 1024

TOKEN_TILE = 512
HALO = V7X_BF16_SUBLANES
FF_CHUNK = 256
XBC_CHUNK = 256
DFT_ROW_TILE = 256
PACK = 16


def _layer_norm(v, g, b):
    mu = jnp.mean(v, axis=-1, keepdims=True)
    vc = v - mu
    var = jnp.mean(vc * vc, axis=-1, keepdims=True)
    return vc * lax.rsqrt(var + LN_EPS) * g + b


def _silu(v):
    return v / (1.0 + jnp.exp(-v))


def _dot(a, b):
    return jnp.dot(a, b, preferred_element_type=F32)


def _params(*semantics):
    return pltpu.CompilerParams(dimension_semantics=semantics,
                                vmem_limit_bytes=V7X_VMEM_LIMIT_BYTES)


def _const_spec(shape):
    zeros = (0,) * len(shape)
    return pl.BlockSpec(shape, lambda *_: zeros)


def _fnet_in_kernel(x_ref, g_ref, b_ref, w_ref, bi_ref, cs_ref, h_ref, z_ref, *, groups):
    h = _layer_norm(x_ref[...], g_ref[...], b_ref[...])
    h_ref[...] = h
    u = (_dot(h.astype(BF16), w_ref[...]) + bi_ref[...]).astype(BF16)
    gd = u.shape[1] // groups
    for g in range(groups):
        r = _dot(u[:, g * gd:(g + 1) * gd], cs_ref[...])
        z_ref[0, :, g * gd:(g + 1) * gd] = r[:, :gd].astype(BF16)
        z_ref[1, :, g * gd:(g + 1) * gd] = r[:, gd:].astype(BF16)


def _fnet_in(x, ln_g, ln_b, w_in, b_in, cs):
    bsz, s, d = x.shape
    tm = min(TOKEN_TILE, s)
    nt = s // tm
    return pl.pallas_call(
        functools.partial(_fnet_in_kernel, groups=FN_GROUPS),
        grid=(bsz, nt),
        in_specs=[
            pl.BlockSpec((None, tm, d), lambda b, i: (b, i, 0)),
            _const_spec((1, d)), _const_spec((1, d)),
            _const_spec(w_in.shape), _const_spec((1, d)), _const_spec(cs.shape),
        ],
        out_specs=[
            pl.BlockSpec((None, tm, d), lambda b, i: (b, i, 0)),
            pl.BlockSpec((None, 2, tm, d), lambda b, i: (b, 0, i, 0)),
        ],
        out_shape=[jax.ShapeDtypeStruct((bsz, s, d), F32),
                   jax.ShapeDtypeStruct((bsz, 2, s, d), BF16)],
        compiler_params=_params("parallel", "parallel"),
        name="fnet_in",
    )(x, ln_g.reshape(1, d), ln_b.reshape(1, d), w_in, b_in.reshape(1, d), cs)


def _fnet_seq_kernel(t_ref, z_ref, h_ref, w_ref, bo_ref, g_ref, b_ref, o_ref):
    y = _dot(t_ref[...], z_ref[...])
    mix = _dot(y.astype(BF16), w_ref[...]) + bo_ref[...]
    o_ref[...] = _layer_norm(DEEPNORM_ALPHA * h_ref[...] + mix, g_ref[...], b_ref[...])


def _fnet_seq(table, z, h, w_out, b_out, ln_g, ln_b):
    bsz, s, d = h.shape
    tr = min(DFT_ROW_TILE, s)
    z2 = z.reshape(bsz, 2 * s, d)
    return pl.pallas_call(
        _fnet_seq_kernel,
        grid=(bsz, s // tr),
        in_specs=[
            pl.BlockSpec((tr, 2 * s), lambda b, i: (i, 0)),
            pl.BlockSpec((None, 2 * s, d), lambda b, i: (b, 0, 0)),
            pl.BlockSpec((None, tr, d), lambda b, i: (b, i, 0)),
            _const_spec(w_out.shape), _const_spec((1, d)), _const_spec((1, d)), _const_spec((1, d)),
        ],
        out_specs=pl.BlockSpec((None, tr, d), lambda b, i: (b, i, 0)),
        out_shape=jax.ShapeDtypeStruct((bsz, s, d), F32),
        compiler_params=_params("parallel", "arbitrary"),
        name="fnet_seq",
    )(table, z2, h, w_out, b_out.reshape(1, d), ln_g.reshape(1, d), ln_b.reshape(1, d))


def _assemble_halo(xb_ref, hp_ref, hm, hn_ref, tiles_per_seq):
    i = pl.program_id(0)
    tm = hm.shape[0]
    pos = lax.rem(i, tiles_per_seq)
    keep_prev = (pos != 0).astype(F32)
    keep_next = (pos != tiles_per_seq - 1).astype(F32)
    xb_ref[0:HALO, :] = (hp_ref[...] * keep_prev).astype(BF16)
    xb_ref[HALO:HALO + tm, :] = hm.astype(BF16)
    xb_ref[HALO + tm:, :] = (hn_ref[...] * keep_next).astype(BF16)


def _halo_specs(tm, d, n_rows):
    per = tm // HALO
    last = n_rows // HALO - 1
    return [
        pl.BlockSpec((HALO, d), lambda i: (jnp.maximum(i * per - 1, 0), 0)),
        pl.BlockSpec((tm, d), lambda i: (i, 0)),
        pl.BlockSpec((HALO, d), lambda i: (jnp.minimum((i + 1) * per, last), 0)),
    ]


def _conv_rows(u_ref, w, bias, tm, width):
    half = width // 2
    out = bias
    for k in range(width):
        out = out + w[k:k + 1, :] * u_ref[HALO - half + k:HALO - half + k + tm, :]
    return out


def _conv_ffn_kernel(hp_ref, hm_ref, hn_ref, wg_ref, wv_ref, cwg_ref, cwv_ref, cbg_ref, cbv_ref,
                     wd_ref, g_ref, b_ref, o_ref, xb_ref, ug_ref, uv_ref, acc_ref,
                     *, tiles_per_seq, n_chunks, width):
    hm = hm_ref[...]
    tm = hm.shape[0]
    _assemble_halo(xb_ref, hp_ref, hm, hn_ref, tiles_per_seq)
    acc_ref[...] = jnp.zeros_like(acc_ref)

    def body(c, carry):
        xb = xb_ref[...]
        ug_ref[...] = _dot(xb, wg_ref[c])
        uv_ref[...] = _dot(xb, wv_ref[c])
        cg = _conv_rows(ug_ref, cwg_ref[c], cbg_ref[c], tm, width)
        cv = _conv_rows(uv_ref, cwv_ref[c], cbv_ref[c], tm, width)
        act = (_silu(cg) * cv).astype(BF16)
        acc_ref[...] += _dot(act, wd_ref[c])
        return carry

    lax.fori_loop(0, n_chunks, body, 0)
    o_ref[...] = _layer_norm(DEEPNORM_ALPHA * hm + acc_ref[...], g_ref[...], b_ref[...])


def _conv_ffn(h, seq_len, w_up, conv_w, conv_b, w_down, ln_g, ln_b):
    n_rows, d = h.shape
    d_ff = w_down.shape[0]
    width = conv_w.shape[0]
    tm = min(TOKEN_TILE, seq_len)
    cw = FF_CHUNK
    nc = d_ff // cw
    chunked = lambda a, rows: a.reshape(rows, nc, cw).transpose(1, 0, 2)
    wg = chunked(w_up[:, :d_ff], d).astype(BF16)
    wv = chunked(w_up[:, d_ff:], d).astype(BF16)
    cwg = chunked(conv_w[:, :d_ff], width)
    cwv = chunked(conv_w[:, d_ff:], width)
    cbg = conv_b[:d_ff].reshape(nc, 1, cw)
    cbv = conv_b[d_ff:].reshape(nc, 1, cw)
    wd = w_down.reshape(nc, cw, d).astype(BF16)
    return pl.pallas_call(
        functools.partial(_conv_ffn_kernel, tiles_per_seq=seq_len // tm, n_chunks=nc, width=width),
        grid=(n_rows // tm,),
        in_specs=_halo_specs(tm, d, n_rows) + [
            _const_spec(wg.shape), _const_spec(wv.shape), _const_spec(cwg.shape), _const_spec(cwv.shape),
            _const_spec(cbg.shape), _const_spec(cbv.shape), _const_spec(wd.shape),
            _const_spec((1, d)), _const_spec((1, d)),
        ],
        out_specs=pl.BlockSpec((tm, d), lambda i: (i, 0)),
        out_shape=jax.ShapeDtypeStruct((n_rows, d), F32),
        scratch_shapes=[
            pltpu.VMEM((tm + 2 * HALO, d), BF16),
            pltpu.VMEM((tm + 2 * HALO, cw), F32),
            pltpu.VMEM((tm + 2 * HALO, cw), F32),
            pltpu.VMEM((tm, d), F32),
        ],
        compiler_params=_params("parallel"),
        name="conv_ffn",
    )(h, h, h, wg, wv, cwg, cwv, cbg, cbv, wd, ln_g.reshape(1, d), ln_b.reshape(1, d))


def _ssd_in_kernel(hp_ref, hm_ref, hn_ref, wz_ref, wx_ref, cw_ref, cb_ref, wdt_ref, dtb_ref, alog_ref,
                   z_ref, xbc_ref, pa_ref, pb_ref, pt_ref, xb_ref, u_ref,
                   *, tiles_per_seq, n_chunks, width):
    hm = hm_ref[...]
    tm = hm.shape[0]
    cw = u_ref.shape[1]
    _assemble_halo(xb_ref, hp_ref, hm, hn_ref, tiles_per_seq)
    xm = xb_ref[HALO:HALO + tm, :]
    z_ref[...] = _dot(xm, wz_ref[...]).astype(BF16)

    def body(c, carry):
        u_ref[...] = _dot(xb_ref[...], wx_ref[c])
        v = _conv_rows(u_ref, cw_ref[c], cb_ref[c], tm, width)
        xbc_ref[:, pl.ds(pl.multiple_of(c * cw, cw), cw)] = _silu(v).astype(BF16)
        return carry

    lax.fori_loop(0, n_chunks, body, 0)

    raw = _dot(xm, wdt_ref[...]) + dtb_ref[...]
    dt = jnp.maximum(raw, 0.0) + jnp.log(1.0 + jnp.exp(-jnp.abs(raw)))
    a = -jnp.exp(alog_ref[...])
    dta = dt * a
    lane = lax.broadcasted_iota(jnp.int32, (1, V7X_LANES), 1)
    is_cum = (lane & 8) != 0
    is_bwd = (lane & 4) != 0
    row = lax.broadcasted_iota(jnp.int32, (CHUNK, CHUNK), 0)
    col = lax.broadcasted_iota(jnp.int32, (CHUNK, CHUNK), 1)
    tri_f = (row >= col).astype(F32)
    tri_b = (row <= col).astype(F32)
    for k in range(tm // CHUNK):
        rows = slice(k * CHUNK, (k + 1) * CHUNK)
        dk = dta[rows, :]
        cf = jnp.dot(tri_f, dk, precision=lax.Precision.HIGHEST, preferred_element_type=F32)
        cb = jnp.dot(tri_b, dk, precision=lax.Precision.HIGHEST, preferred_element_type=F32)
        cum = jnp.where(is_bwd, cb, cf)
        end = jnp.where(is_bwd, cum[0:1, :], cum[CHUNK - 1:CHUNK, :])
        dtk = dt[rows, :]
        pa = jnp.where(is_cum, cum, dtk)
        pa_ref[rows, :] = pa
        pb_ref[rows, :] = jnp.where(is_cum, dtk * jnp.exp(end - cum), jnp.exp(cum))
        pt_ref[:, rows] = pa.T


def _ssd_in(h, seq_len, w_z, w_xbc, conv_w, conv_b, w_dt, dt_bias, a_log):
    n_rows, d = h.shape
    d_inner = w_z.shape[1]
    conv_dim = w_xbc.shape[1]
    width = conv_w.shape[0]
    tm = min(TOKEN_TILE, seq_len)
    cw = XBC_CHUNK
    nc = conv_dim // cw
    wx = w_xbc.reshape(d, nc, cw).transpose(1, 0, 2).astype(BF16)
    cwc = conv_w.reshape(width, nc, cw).transpose(1, 0, 2)
    cbc = conv_b.reshape(nc, 1, cw)
    return pl.pallas_call(
        functools.partial(_ssd_in_kernel, tiles_per_seq=seq_len // tm, n_chunks=nc, width=width),
        grid=(n_rows // tm,),
        in_specs=_halo_specs(tm, d, n_rows) + [
            _const_spec(w_z.shape), _const_spec(wx.shape), _const_spec(cwc.shape), _const_spec(cbc.shape),
            _const_spec(w_dt.shape), _const_spec((1, V7X_LANES)), _const_spec((1, V7X_LANES)),
        ],
        out_specs=[
            pl.BlockSpec((tm, d_inner), lambda i: (i, 0)),
            pl.BlockSpec((tm, conv_dim), lambda i: (i, 0)),
            pl.BlockSpec((tm, V7X_LANES), lambda i: (i, 0)),
            pl.BlockSpec((tm, V7X_LANES), lambda i: (i, 0)),
            pl.BlockSpec((V7X_LANES, tm), lambda i: (0, i)),
        ],
        out_shape=[
            jax.ShapeDtypeStruct((n_rows, d_inner), BF16),
            jax.ShapeDtypeStruct((n_rows, conv_dim), BF16),
            jax.ShapeDtypeStruct((n_rows, V7X_LANES), F32),
            jax.ShapeDtypeStruct((n_rows, V7X_LANES), F32),
            jax.ShapeDtypeStruct((V7X_LANES, n_rows), F32),
        ],
        scratch_shapes=[
            pltpu.VMEM((tm + 2 * HALO, d), BF16),
            pltpu.VMEM((tm + 2 * HALO, cw), F32),
        ],
        compiler_params=_params("parallel"),
        name="ssd_in",
    )(h, h, h, w_z.astype(BF16), wx, cwc, cbc, w_dt.astype(BF16), dt_bias, a_log)


def _ssd_scan_kernel(x_ref, b_ref, c_ref, z_ref, pa_ref, pb_ref, pt_ref, dsk_ref, ng_ref, o_ref,
                     y_ref, sb_ref, *, n_chunks, heads):
    q = CHUNK
    p = HEAD_DIM
    w = heads * p
    g = pl.program_id(1)
    shift = lax.rem(V7X_LANES - PACK * g, V7X_LANES)
    row = lax.broadcasted_iota(jnp.int32, (q, q), 0)
    col = lax.broadcasted_iota(jnp.int32, (q, q), 1)
    lower = row >= col
    upper = row <= col
    neg_inf = jnp.float32(-jnp.inf)

    def expand(src, lo):
        return jnp.concatenate(
            [jnp.broadcast_to(src[:, lo + r:lo + r + 1], (q, p)) for r in range(heads)], axis=1)

    def pass1(c, hf):
        rows = pl.ds(pl.multiple_of(c * q, q), q)
        xc = x_ref[rows, :]
        bc = b_ref[rows, :]
        cc = c_ref[rows, :]
        pa = pltpu.roll(pa_ref[rows, :], shift, 1)
        pb = pltpu.roll(pb_ref[rows, :], shift, 1)
        pt = pt_ref[:, rows]
        cbm = lax.dot_general(cc, bc, (((1,), (1,)), ((), ())), preferred_element_type=F32)
        yd = []
        for r in range(heads):
            seg_f = pa[:, 8 + r:9 + r] - pt[8 + r:9 + r, :]
            seg_b = pa[:, 12 + r:13 + r] - pt[12 + r:13 + r, :]
            lf = jnp.exp(jnp.where(lower, seg_f, neg_inf)) * pt[r:r + 1, :]
            lb = jnp.exp(jnp.where(upper, seg_b, neg_inf)) * pt[4 + r:5 + r, :]
            m = (cbm * (lf + lb)).astype(BF16)
            yd.append(_dot(m, xc[:, r * p:(r + 1) * p]))
        xf = xc.astype(F32)
        xdec = jnp.concatenate([(xf * expand(pb, 8)).astype(BF16),
                                (xf * expand(pb, 12)).astype(BF16)], axis=1)
        st = lax.dot_general(bc, xdec, (((0,), (0,)), ((), ())), preferred_element_type=F32)
        ef = expand(pb, 0)
        y_ref[rows, :] = (jnp.concatenate(yd, axis=1) + _dot(cc, hf.astype(BF16)) * ef
                          + dsk_ref[...] * xf)
        sb_ref[c] = st[:, w:]
        return hf * ef[q - 1:q, :] + st[:, :w]

    def pass2(k, hb):
        c = n_chunks - 1 - k
        rows = pl.ds(pl.multiple_of(c * q, q), q)
        cc = c_ref[rows, :]
        pb = pltpu.roll(pb_ref[rows, :], shift, 1)
        eb = expand(pb, 4)
        y = y_ref[rows, :] + _dot(cc, hb.astype(BF16)) * eb
        y = y * _silu(z_ref[rows, :].astype(F32))
        y = y * lax.rsqrt(jnp.mean(y * y, axis=-1, keepdims=True) + LN_EPS)
        o_ref[rows, :] = (y * ng_ref[...]).astype(BF16)
        return hb * eb[0:1, :] + sb_ref[c]

    zero = jnp.zeros((D_STATE, w), F32)
    lax.fori_loop(0, n_chunks, pass1, zero)
    lax.fori_loop(0, n_chunks, pass2, zero)


def _ssd_scan(z, xbc, pack_a, pack_b, pack_t, d_skip, norm_g, bsz, seq_len):
    d_inner = z.shape[1]
    groups = SSM_GROUPS
    w = d_inner // groups
    heads = w // HEAD_DIM
    n = D_STATE
    nc = seq_len // CHUNK
    z3 = z.reshape(bsz, seq_len, d_inner)
    xbc3 = xbc.reshape(bsz, seq_len, xbc.shape[1])
    pa3 = pack_a.reshape(bsz, seq_len, V7X_LANES)
    pb3 = pack_b.reshape(bsz, seq_len, V7X_LANES)
    b_off = d_inner // n
    c_off = b_off + groups
    return pl.pallas_call(
        functools.partial(_ssd_scan_kernel, n_chunks=nc, heads=heads),
        grid=(bsz, groups),
        in_specs=[
            pl.BlockSpec((None, seq_len, w), lambda b, g: (b, 0, g)),
            pl.BlockSpec((None, seq_len, n), lambda b, g: (b, 0, b_off + g)),
            pl.BlockSpec((None, seq_len, n), lambda b, g: (b, 0, c_off + g)),
            pl.BlockSpec((None, seq_len, w), lambda b, g: (b, 0, g)),
            pl.BlockSpec((None, seq_len, V7X_LANES), lambda b, g: (b, 0, 0)),
            pl.BlockSpec((None, seq_len, V7X_LANES), lambda b, g: (b, 0, 0)),
            pl.BlockSpec((PACK, seq_len), lambda b, g: (g, b)),
            pl.BlockSpec((1, w), lambda b, g: (0, g)),
            pl.BlockSpec((1, w), lambda b, g: (0, g)),
        ],
        out_specs=pl.BlockSpec((None, seq_len, w), lambda b, g: (b, 0, g)),
        out_shape=jax.ShapeDtypeStruct((bsz, seq_len, d_inner), BF16),
        scratch_shapes=[
            pltpu.VMEM((seq_len, w), F32),
            pltpu.VMEM((nc, n, w), F32),
        ],
        compiler_params=_params("parallel", "arbitrary"),
        name="ssd_scan",
    )(xbc3, xbc3, xbc3, z3, pa3, pb3, pack_t, d_skip, norm_g.reshape(1, d_inner))


def _proj_ln_kernel(y_ref, h_ref, w_ref, g_ref, b_ref, o_ref):
    mix = _dot(y_ref[...], w_ref[...])
    o_ref[...] = _layer_norm(DEEPNORM_ALPHA * h_ref[...] + mix, g_ref[...], b_ref[...])


def _proj_ln(y, h, w, ln_g, ln_b):
    n_rows, d = h.shape
    k = y.shape[1]
    tm = min(TOKEN_TILE, n_rows)
    return pl.pallas_call(
        _proj_ln_kernel,
        grid=(n_rows // tm,),
        in_specs=[
            pl.BlockSpec((tm, k), lambda i: (i, 0)),
            pl.BlockSpec((tm, d), lambda i: (i, 0)),
            _const_spec(w.shape), _const_spec((1, d)), _const_spec((1, d)),
        ],
        out_specs=pl.BlockSpec((tm, d), lambda i: (i, 0)),
        out_shape=jax.ShapeDtypeStruct((n_rows, d), F32),
        compiler_params=_params("parallel"),
        name="proj_ln",
    )(y, h, w.astype(BF16), ln_g.reshape(1, d), ln_b.reshape(1, d))


def _channel_dft_table(gd):
    k = np.arange(gd)
    ang = 2.0 * np.pi * ((k[:, None] * k[None, :]) % gd) / gd
    t = np.concatenate([np.cos(ang), np.sin(ang)], axis=1) / math.sqrt(gd)
    return jnp.asarray(t, dtype=BF16)


def _seq_dft_table(s):
    k = jnp.arange(s, dtype=jnp.int32)
    ang = (2.0 * math.pi / s) * ((k[:, None] * k[None, :]) % s).astype(F32)
    scale = 1.0 / math.sqrt(s)
    return jnp.concatenate([jnp.cos(ang) * scale, jnp.sin(ang) * (-scale)], axis=1).astype(BF16)


def _packed_lane_heads():
    lane = np.arange(V7X_LANES)
    g, d, r = lane // PACK, (lane // 4) % 2, lane % 4
    return d, g * 4 + r


def kernel(x, emb_ln_g, emb_ln_b, fn_w_in, fn_b_in, fn_w_out, fn_b_out, ssd_w_in, ssd_conv_w, ssd_conv_b, ssd_a_log_fwd, ssd_a_log_bwd, ssd_dt_bias_fwd, ssd_dt_bias_bwd, ssd_d, ssd_norm_g, ssd_w_out, ln_tok_g, ln_tok_b, ff_w_up, ff_conv_w, ff_conv_b, ff_w_down, ln_ffn_g, ln_ffn_b):
    bsz, s, d = x.shape
    n_rows = bsz * s
    d_inner = SSM_EXPAND * d
    n_heads = d_inner // HEAD_DIM
    conv_dim = d_inner + 2 * SSM_GROUPS * D_STATE
    assert s % TOKEN_TILE == 0 and s % CHUNK == 0 and d % (FN_GROUPS * V7X_LANES) == 0
    assert n_heads // SSM_GROUPS == 4 and SSM_GROUPS * PACK == V7X_LANES

    h = None
    for i in range(DEPTH):
        j = i // N_MIXERS
        if i % N_MIXERS == 0:
            cs = _channel_dft_table(d // FN_GROUPS)
            table = _seq_dft_table(s)
            if h is None:
                h0, z = _fnet_in(x, emb_ln_g, emb_ln_b, fn_w_in[j].astype(BF16), fn_b_in[j], cs)
            else:
                raise NotImplementedError("only the first layer uses the Fourier mixer at this depth")
            h = _fnet_seq(table, z, h0, fn_w_out[j].astype(BF16), fn_b_out[j],
                          ln_tok_g[i], ln_tok_b[i]).reshape(n_rows, d)
        else:
            w_in = ssd_w_in[j]
            lane_dir, lane_head = _packed_lane_heads()
            dt_cols = d_inner + conv_dim + lane_dir * n_heads + lane_head
            w_dt = w_in[:, dt_cols]
            pick = lambda f, b: jnp.where(jnp.asarray(lane_dir == 1), b[lane_head], f[lane_head]).reshape(1, V7X_LANES)
            z, xbc, pack_a, pack_b, pack_t = _ssd_in(
                h, s, w_in[:, :d_inner], w_in[:, d_inner:d_inner + conv_dim], ssd_conv_w[j], ssd_conv_b[j],
                w_dt, pick(ssd_dt_bias_fwd[j], ssd_dt_bias_bwd[j]), pick(ssd_a_log_fwd[j], ssd_a_log_bwd[j]))
            d_skip = jnp.repeat(ssd_d[j], HEAD_DIM).reshape(1, d_inner)
            yn = _ssd_scan(z, xbc, pack_a, pack_b, pack_t, d_skip, ssd_norm_g[j], bsz, s)
            h = _proj_ln(yn.reshape(n_rows, d_inner), h, ssd_w_out[j], ln_tok_g[i], ln_tok_b[i])
        h = _conv_ffn(h, s, ff_w_up[i], ff_conv_w[i], ff_conv_b[i], ff_w_down[i], ln_ffn_g[i], ln_ffn_b[i])
    return h.reshape(bsz, s, d)
```

```python
import functools
import math

import numpy as np
import jax
import jax.numpy as jnp
from jax import lax
from jax.experimental import pallas as pl
from jax.experimental.pallas import tpu as pltpu

F32 = jnp.float32
BF16 = jnp.bfloat16

LN_EPS = 1e-5
DEPTH = 2
N_MIXERS = 2
DEEPNORM_ALPHA = (2.0 * DEPTH) ** 0.25
FN_GROUPS = 8
HEAD_DIM = 64
D_STATE = 128
SSM_GROUPS = 8
SSM_EXPAND = 2
CHUNK = 128

V7X_LANES = 128
V7X_BF16_SUBLANES = 16
V7X_VMEM_LIMIT_BYTES = 56 * 1024 * 1024

TOKEN_TILE = 512
HALO = V7X_BF16_SUBLANES
FF_CHUNK = 256
XBC_CHUNK = 256
DFT_ROW_TILE = 256
PACK = 16


def _layer_norm(v, g, b):
    mu = jnp.mean(v, axis=-1, keepdims=True)
    vc = v - mu
    var = jnp.mean(vc * vc, axis=-1, keepdims=True)
    return vc * lax.rsqrt(var + LN_EPS) * g + b


def _silu(v):
    hv = 0.5 * v
    return hv + hv * jnp.tanh(hv)


def _dot(a, b):
    return jnp.dot(a, b, preferred_element_type=F32)


def _params(*semantics):
    return pltpu.CompilerParams(dimension_semantics=semantics,
                                vmem_limit_bytes=V7X_VMEM_LIMIT_BYTES)


def _const_spec(shape):
    zeros = (0,) * len(shape)
    return pl.BlockSpec(shape, lambda *_: zeros)


def _fnet_in_kernel(x_ref, g_ref, b_ref, w_ref, bi_ref, cs_ref, h_ref, z_ref, *, groups):
    h = _layer_norm(x_ref[...], g_ref[...], b_ref[...])
    h_ref[...] = h
    u = (_dot(h.astype(BF16), w_ref[...]) + bi_ref[...]).astype(BF16)
    gd = u.shape[1] // groups
    for g in range(groups):
        r = _dot(u[:, g * gd:(g + 1) * gd], cs_ref[...])
        z_ref[0, :, g * gd:(g + 1) * gd] = r[:, :gd].astype(BF16)
        z_ref[1, :, g * gd:(g + 1) * gd] = r[:, gd:].astype(BF16)


def _fnet_in(x, ln_g, ln_b, w_in, b_in, cs):
    bsz, s, d = x.shape
    tm = min(TOKEN_TILE, s)
    nt = s // tm
    return pl.pallas_call(
        functools.partial(_fnet_in_kernel, groups=FN_GROUPS),
        grid=(bsz, nt),
        in_specs=[
            pl.BlockSpec((None, tm, d), lambda b, i: (b, i, 0)),
            _const_spec((1, d)), _const_spec((1, d)),
            _const_spec(w_in.shape), _const_spec((1, d)), _const_spec(cs.shape),
        ],
        out_specs=[
            pl.BlockSpec((None, tm, d), lambda b, i: (b, i, 0)),
            pl.BlockSpec((None, 2, tm, d), lambda b, i: (b, 0, i, 0)),
        ],
        out_shape=[jax.ShapeDtypeStruct((bsz, s, d), F32),
                   jax.ShapeDtypeStruct((bsz, 2, s, d), BF16)],
        compiler_params=_params("parallel", "parallel"),
        name="fnet_in",
    )(x, ln_g.reshape(1, d), ln_b.reshape(1, d), w_in, b_in.reshape(1, d), cs)


def _fnet_seq_kernel(t_ref, z_ref, h_ref, w_ref, bo_ref, g_ref, b_ref, o_ref):
    y = _dot(t_ref[...], z_ref[...])
    mix = _dot(y.astype(BF16), w_ref[...]) + bo_ref[...]
    o_ref[...] = _layer_norm(DEEPNORM_ALPHA * h_ref[...] + mix, g_ref[...], b_ref[...])


def _fnet_seq(table, z, h, w_out, b_out, ln_g, ln_b):
    bsz, s, d = h.shape
    tr = min(DFT_ROW_TILE, s)
    z2 = z.reshape(bsz, 2 * s, d)
    return pl.pallas_call(
        _fnet_seq_kernel,
        grid=(bsz, s // tr),
        in_specs=[
            pl.BlockSpec((tr, 2 * s), lambda b, i: (i, 0)),
            pl.BlockSpec((None, 2 * s, d), lambda b, i: (b, 0, 0)),
            pl.BlockSpec((None, tr, d), lambda b, i: (b, i, 0)),
            _const_spec(w_out.shape), _const_spec((1, d)), _const_spec((1, d)), _const_spec((1, d)),
        ],
        out_specs=pl.BlockSpec((None, tr, d), lambda b, i: (b, i, 0)),
        out_shape=jax.ShapeDtypeStruct((bsz, s, d), F32),
        compiler_params=_params("parallel", "arbitrary"),
        name="fnet_seq",
    )(table, z2, h, w_out, b_out.reshape(1, d), ln_g.reshape(1, d), ln_b.reshape(1, d))


def _assemble_halo(xb_ref, hp_ref, hm, hn_ref, tiles_per_seq):
    i = pl.program_id(0)
    tm = hm.shape[0]
    pos = lax.rem(i, tiles_per_seq)
    keep_prev = (pos != 0).astype(F32)
    keep_next = (pos != tiles_per_seq - 1).astype(F32)
    xb_ref[0:HALO, :] = (hp_ref[...] * keep_prev).astype(BF16)
    xb_ref[HALO:HALO + tm, :] = hm.astype(BF16)
    xb_ref[HALO + tm:, :] = (hn_ref[...] * keep_next).astype(BF16)


def _halo_specs(tm, d, n_rows):
    per = tm // HALO
    last = n_rows // HALO - 1
    return [
        pl.BlockSpec((HALO, d), lambda i: (jnp.maximum(i * per - 1, 0), 0)),
        pl.BlockSpec((tm, d), lambda i: (i, 0)),
        pl.BlockSpec((HALO, d), lambda i: (jnp.minimum((i + 1) * per, last), 0)),
    ]


def _conv_rows(u_ref, w, bias, tm, width):
    half = width // 2
    u = u_ref[...]
    n = u.shape[0]
    out = w[half:half + 1, :] * u[HALO:HALO + tm] + bias
    for k in range(width):
        if k != half:
            shifted = pltpu.roll(u, (half - k) % n, 0)[HALO:HALO + tm]
            out = out + w[k:k + 1, :] * shifted
    return out


def _conv_ffn_kernel(hp_ref, hm_ref, hn_ref, wg_ref, wv_ref, cwg_ref, cwv_ref, cbg_ref, cbv_ref,
                     wd_ref, g_ref, b_ref, o_ref, xb_ref, ug_ref, uv_ref, act_ref,
                     *, tiles_per_seq, n_chunks, width):
    hm = hm_ref[...]
    tm = hm.shape[0]
    cw = ug_ref.shape[2]
    _assemble_halo(xb_ref, hp_ref, hm, hn_ref, tiles_per_seq)
    for c in range(n_chunks):
        slot = c % 2
        xb = xb_ref[...]
        ug_ref[slot] = _dot(xb, wg_ref[c])
        uv_ref[slot] = _dot(xb, wv_ref[c])
        cg = _conv_rows(ug_ref.at[slot], cwg_ref[c], cbg_ref[c], tm, width)
        cv = _conv_rows(uv_ref.at[slot], cwv_ref[c], cbv_ref[c], tm, width)
        act_ref[:, c * cw:(c + 1) * cw] = (_silu(cg) * cv).astype(BF16)
    ffn = _dot(act_ref[...], wd_ref[...])
    o_ref[...] = _layer_norm(DEEPNORM_ALPHA * hm + ffn, g_ref[...], b_ref[...])


def _conv_ffn(h, seq_len, w_up, conv_w, conv_b, w_down, ln_g, ln_b):
    n_rows, d = h.shape
    d_ff = w_down.shape[0]
    width = conv_w.shape[0]
    tm = min(TOKEN_TILE, seq_len)
    cw = FF_CHUNK
    nc = d_ff // cw
    chunked = lambda a, rows: a.reshape(rows, nc, cw).transpose(1, 0, 2)
    wg = chunked(w_up[:, :d_ff], d).astype(BF16)
    wv = chunked(w_up[:, d_ff:], d).astype(BF16)
    cwg = chunked(conv_w[:, :d_ff], width)
    cwv = chunked(conv_w[:, d_ff:], width)
    cbg = conv_b[:d_ff].reshape(nc, 1, cw)
    cbv = conv_b[d_ff:].reshape(nc, 1, cw)
    wd = w_down.astype(BF16)
    return pl.pallas_call(
        functools.partial(_conv_ffn_kernel, tiles_per_seq=seq_len // tm, n_chunks=nc, width=width),
        grid=(n_rows // tm,),
        in_specs=_halo_specs(tm, d, n_rows) + [
            _const_spec(wg.shape), _const_spec(wv.shape), _const_spec(cwg.shape), _const_spec(cwv.shape),
            _const_spec(cbg.shape), _const_spec(cbv.shape), _const_spec(wd.shape),
            _const_spec((1, d)), _const_spec((1, d)),
        ],
        out_specs=pl.BlockSpec((tm, d), lambda i: (i, 0)),
        out_shape=jax.ShapeDtypeStruct((n_rows, d), F32),
        scratch_shapes=[
            pltpu.VMEM((tm + 2 * HALO, d), BF16),
            pltpu.VMEM((2, tm + 2 * HALO, cw), F32),
            pltpu.VMEM((2, tm + 2 * HALO, cw), F32),
            pltpu.VMEM((tm, d_ff), BF16),
        ],
        compiler_params=_params("parallel"),
        name="conv_ffn",
    )(h, h, h, wg, wv, cwg, cwv, cbg, cbv, wd, ln_g.reshape(1, d), ln_b.reshape(1, d))


def _ssd_in_kernel(hp_ref, hm_ref, hn_ref, wz_ref, wx_ref, cw_ref, cb_ref, wdt_ref, dtb_ref, alog_ref,
                   z_ref, xbc_ref, pa_ref, pb_ref, pt_ref, xb_ref, u_ref,
                   *, tiles_per_seq, n_chunks, width):
    hm = hm_ref[...]
    tm = hm.shape[0]
    cw = u_ref.shape[2]
    _assemble_halo(xb_ref, hp_ref, hm, hn_ref, tiles_per_seq)
    xm = xb_ref[HALO:HALO + tm, :]
    z_ref[...] = _dot(xm, wz_ref[...]).astype(BF16)

    for c in range(n_chunks):
        slot = c % 2
        u_ref[slot] = _dot(xb_ref[...], wx_ref[c])
        v = _conv_rows(u_ref.at[slot], cw_ref[c], cb_ref[c], tm, width)
        xbc_ref[:, c * cw:(c + 1) * cw] = _silu(v).astype(BF16)

    raw = _dot(xm, wdt_ref[...]) + dtb_ref[...]
    dt = jnp.maximum(raw, 0.0) + jnp.log(1.0 + jnp.exp(-jnp.abs(raw)))
    a = -jnp.exp(alog_ref[...])
    dta = dt * a
    lane = lax.broadcasted_iota(jnp.int32, (1, V7X_LANES), 1)
    is_cum = (lane & 8) != 0
    is_bwd = (lane & 4) != 0
    row = lax.broadcasted_iota(jnp.int32, (CHUNK, CHUNK), 0)
    col = lax.broadcasted_iota(jnp.int32, (CHUNK, CHUNK), 1)
    tri_f = (row >= col).astype(F32)
    tri_b = (row <= col).astype(F32)
    for k in range(tm // CHUNK):
        rows = slice(k * CHUNK, (k + 1) * CHUNK)
        dk = dta[rows, :]
        cf = jnp.dot(tri_f, dk, precision=lax.Precision.HIGHEST, preferred_element_type=F32)
        cb = jnp.dot(tri_b, dk, precision=lax.Precision.HIGHEST, preferred_element_type=F32)
        cum = jnp.where(is_bwd, cb, cf)
        end = jnp.where(is_bwd, cum[0:1, :], cum[CHUNK - 1:CHUNK, :])
        dtk = dt[rows, :]
        pa = jnp.where(is_cum, cum, dtk)
        pa_ref[rows, :] = pa
        pb_ref[rows, :] = jnp.where(is_cum, dtk * jnp.exp(end - cum), jnp.exp(cum))
        pt_ref[:, rows] = pa.T


def _ssd_in(h, seq_len, w_z, w_xbc, conv_w, conv_b, w_dt, dt_bias, a_log):
    n_rows, d = h.shape
    d_inner = w_z.shape[1]
    conv_dim = w_xbc.shape[1]
    width = conv_w.shape[0]
    tm = min(TOKEN_TILE, seq_len)
    cw = XBC_CHUNK
    nc = conv_dim // cw
    wx = w_xbc.reshape(d, nc, cw).transpose(1, 0, 2).astype(BF16)
    cwc = conv_w.reshape(width, nc, cw).transpose(1, 0, 2)
    cbc = conv_b.reshape(nc, 1, cw)
    return pl.pallas_call(
        functools.partial(_ssd_in_kernel, tiles_per_seq=seq_len // tm, n_chunks=nc, width=width),
        grid=(n_rows // tm,),
        in_specs=_halo_specs(tm, d, n_rows) + [
            _const_spec(w_z.shape), _const_spec(wx.shape), _const_spec(cwc.shape), _const_spec(cbc.shape),
            _const_spec(w_dt.shape), _const_spec((1, V7X_LANES)), _const_spec((1, V7X_LANES)),
        ],
        out_specs=[
            pl.BlockSpec((tm, d_inner), lambda i: (i, 0)),
            pl.BlockSpec((tm, conv_dim), lambda i: (i, 0)),
            pl.BlockSpec((tm, V7X_LANES), lambda i: (i, 0)),
            pl.BlockSpec((tm, V7X_LANES), lambda i: (i, 0)),
            pl.BlockSpec((V7X_LANES, tm), lambda i: (0, i)),
        ],
        out_shape=[
            jax.ShapeDtypeStruct((n_rows, d_inner), BF16),
            jax.ShapeDtypeStruct((n_rows, conv_dim), BF16),
            jax.ShapeDtypeStruct((n_rows, V7X_LANES), F32),
            jax.ShapeDtypeStruct((n_rows, V7X_LANES), F32),
            jax.ShapeDtypeStruct((V7X_LANES, n_rows), F32),
        ],
        scratch_shapes=[
            pltpu.VMEM((tm + 2 * HALO, d), BF16),
            pltpu.VMEM((2, tm + 2 * HALO, cw), F32),
        ],
        compiler_params=_params("parallel"),
        name="ssd_in",
    )(h, h, h, w_z.astype(BF16), wx, cwc, cbc, w_dt.astype(BF16), dt_bias, a_log)


def _ssd_scan_kernel(x_ref, b_ref, c_ref, z_ref, pa_ref, pb_ref, pt_ref, dsk_ref, ng_ref, o_ref,
                     y_ref, sb_ref, *, n_chunks, heads):
    q = CHUNK
    p = HEAD_DIM
    w = heads * p
    g = pl.program_id(1)
    shift = lax.rem(V7X_LANES - PACK * g, V7X_LANES)
    row = lax.broadcasted_iota(jnp.int32, (q, q), 0)
    col = lax.broadcasted_iota(jnp.int32, (q, q), 1)
    lower = row >= col
    upper = row <= col
    neg_inf = jnp.float32(-jnp.inf)

    def expand(src, lo):
        return jnp.concatenate(
            [jnp.broadcast_to(src[:, lo + r:lo + r + 1], (q, p)) for r in range(heads)], axis=1)

    def pass1(c, hf):
        rows = pl.ds(pl.multiple_of(c * q, q), q)
        xc = x_ref[rows, :]
        bc = b_ref[rows, :]
        cc = c_ref[rows, :]
        pa = pltpu.roll(pa_ref[rows, :], shift, 1)
        pb = pltpu.roll(pb_ref[rows, :], shift, 1)
        pt = pt_ref[:, rows]
        cbm = lax.dot_general(cc, bc, (((1,), (1,)), ((), ())), preferred_element_type=F32)
        yd = []
        for r in range(heads):
            seg_f = pa[:, 8 + r:9 + r] - pt[8 + r:9 + r, :]
            seg_b = pa[:, 12 + r:13 + r] - pt[12 + r:13 + r, :]
            lf = jnp.exp(jnp.where(lower, seg_f, neg_inf)) * pt[r:r + 1, :]
            lb = jnp.exp(jnp.where(upper, seg_b, neg_inf)) * pt[4 + r:5 + r, :]
            m = (cbm * (lf + lb)).astype(BF16)
            yd.append(_dot(m, xc[:, r * p:(r + 1) * p]))
        xf = xc.astype(F32)
        xdec = jnp.concatenate([(xf * expand(pb, 8)).astype(BF16),
                                (xf * expand(pb, 12)).astype(BF16)], axis=1)
        st = lax.dot_general(bc, xdec, (((0,), (0,)), ((), ())), preferred_element_type=F32)
        ef = expand(pb, 0)
        y_ref[rows, :] = (jnp.concatenate(yd, axis=1) + _dot(cc, hf.astype(BF16)) * ef
                          + dsk_ref[...] * xf)
        sb_ref[c] = st[:, w:]
        return hf * ef[q - 1:q, :] + st[:, :w]

    def pass2(k, hb):
        c = n_chunks - 1 - k
        rows = pl.ds(pl.multiple_of(c * q, q), q)
        cc = c_ref[rows, :]
        pb = pltpu.roll(pb_ref[rows, :], shift, 1)
        eb = expand(pb, 4)
        y = y_ref[rows, :] + _dot(cc, hb.astype(BF16)) * eb
        y = y * _silu(z_ref[rows, :].astype(F32))
        y = y * lax.rsqrt(jnp.mean(y * y, axis=-1, keepdims=True) + LN_EPS)
        o_ref[rows, :] = (y * ng_ref[...]).astype(BF16)
        return hb * eb[0:1, :] + sb_ref[c]

    zero = jnp.zeros((D_STATE, w), F32)
    lax.fori_loop(0, n_chunks, pass1, zero)
    lax.fori_loop(0, n_chunks, pass2, zero)


def _ssd_scan(z, xbc, pack_a, pack_b, pack_t, d_skip, norm_g, bsz, seq_len):
    d_inner = z.shape[1]
    groups = SSM_GROUPS
    w = d_inner // groups
    heads = w // HEAD_DIM
    n = D_STATE
    nc = seq_len // CHUNK
    z3 = z.reshape(bsz, seq_len, d_inner)
    xbc3 = xbc.reshape(bsz, seq_len, xbc.shape[1])
    pa3 = pack_a.reshape(bsz, seq_len, V7X_LANES)
    pb3 = pack_b.reshape(bsz, seq_len, V7X_LANES)
    b_off = d_inner // n
    c_off = b_off + groups
    return pl.pallas_call(
        functools.partial(_ssd_scan_kernel, n_chunks=nc, heads=heads),
        grid=(bsz, groups),
        in_specs=[
            pl.BlockSpec((None, seq_len, w), lambda b, g: (b, 0, g)),
            pl.BlockSpec((None, seq_len, n), lambda b, g: (b, 0, b_off + g)),
            pl.BlockSpec((None, seq_len, n), lambda b, g: (b, 0, c_off + g)),
            pl.BlockSpec((None, seq_len, w), lambda b, g: (b, 0, g)),
            pl.BlockSpec((None, seq_len, V7X_LANES), lambda b, g: (b, 0, 0)),
            pl.BlockSpec((None, seq_len, V7X_LANES), lambda b, g: (b, 0, 0)),
            pl.BlockSpec((PACK, seq_len), lambda b, g: (g, b)),
            pl.BlockSpec((1, w), lambda b, g: (0, g)),
            pl.BlockSpec((1, w), lambda b, g: (0, g)),
        ],
        out_specs=pl.BlockSpec((None, seq_len, w), lambda b, g: (b, 0, g)),
        out_shape=jax.ShapeDtypeStruct((bsz, seq_len, d_inner), BF16),
        scratch_shapes=[
            pltpu.VMEM((seq_len, w), F32),
            pltpu.VMEM((nc, n, w), F32),
        ],
        compiler_params=_params("parallel", "arbitrary"),
        name="ssd_scan",
    )(xbc3, xbc3, xbc3, z3, pa3, pb3, pack_t, d_skip, norm_g.reshape(1, d_inner))


def _proj_ln_kernel(y_ref, h_ref, w_ref, g_ref, b_ref, o_ref):
    mix = _dot(y_ref[...], w_ref[...])
    o_ref[...] = _layer_norm(DEEPNORM_ALPHA * h_ref[...] + mix, g_ref[...], b_ref[...])


def _proj_ln(y, h, w, ln_g, ln_b):
    n_rows, d = h.shape
    k = y.shape[1]
    tm = min(TOKEN_TILE, n_rows)
    return pl.pallas_call(
        _proj_ln_kernel,
        grid=(n_rows // tm,),
        in_specs=[
            pl.BlockSpec((tm, k), lambda i: (i, 0)),
            pl.BlockSpec((tm, d), lambda i: (i, 0)),
            _const_spec(w.shape), _const_spec((1, d)), _const_spec((1, d)),
        ],
        out_specs=pl.BlockSpec((tm, d), lambda i: (i, 0)),
        out_shape=jax.ShapeDtypeStruct((n_rows, d), F32),
        compiler_params=_params("parallel"),
        name="proj_ln",
    )(y, h, w.astype(BF16), ln_g.reshape(1, d), ln_b.reshape(1, d))


def _channel_dft_table(gd):
    k = np.arange(gd)
    ang = 2.0 * np.pi * ((k[:, None] * k[None, :]) % gd) / gd
    t = np.concatenate([np.cos(ang), np.sin(ang)], axis=1) / math.sqrt(gd)
    return jnp.asarray(t, dtype=BF16)


def _seq_dft_table(s):
    k = jnp.arange(s, dtype=jnp.int32)
    ang = (2.0 * math.pi / s) * ((k[:, None] * k[None, :]) % s).astype(F32)
    scale = 1.0 / math.sqrt(s)
    return jnp.concatenate([jnp.cos(ang) * scale, jnp.sin(ang) * (-scale)], axis=1).astype(BF16)


def _packed_lane_heads():
    lane = np.arange(V7X_LANES)
    g, d, r = lane // PACK, (lane // 4) % 2, lane % 4
    return d, g * 4 + r


def kernel(x, emb_ln_g, emb_ln_b, fn_w_in, fn_b_in, fn_w_out, fn_b_out, ssd_w_in, ssd_conv_w, ssd_conv_b, ssd_a_log_fwd, ssd_a_log_bwd, ssd_dt_bias_fwd, ssd_dt_bias_bwd, ssd_d, ssd_norm_g, ssd_w_out, ln_tok_g, ln_tok_b, ff_w_up, ff_conv_w, ff_conv_b, ff_w_down, ln_ffn_g, ln_ffn_b):
    bsz, s, d = x.shape
    n_rows = bsz * s
    d_inner = SSM_EXPAND * d
    n_heads = d_inner // HEAD_DIM
    conv_dim = d_inner + 2 * SSM_GROUPS * D_STATE
    assert s % TOKEN_TILE == 0 and s % CHUNK == 0 and d % (FN_GROUPS * V7X_LANES) == 0
    assert n_heads // SSM_GROUPS == 4 and SSM_GROUPS * PACK == V7X_LANES

    h = None
    for i in range(DEPTH):
        j = i // N_MIXERS
        if i % N_MIXERS == 0:
            cs = _channel_dft_table(d // FN_GROUPS)
            table = _seq_dft_table(s)
            if h is None:
                h0, z = _fnet_in(x, emb_ln_g, emb_ln_b, fn_w_in[j].astype(BF16), fn_b_in[j], cs)
            else:
                raise NotImplementedError("only the first layer uses the Fourier mixer at this depth")
            h = _fnet_seq(table, z, h0, fn_w_out[j].astype(BF16), fn_b_out[j],
                          ln_tok_g[i], ln_tok_b[i]).reshape(n_rows, d)
        else:
            w_in = ssd_w_in[j]
            lane_dir, lane_head = _packed_lane_heads()
            dt_cols = d_inner + conv_dim + lane_dir * n_heads + lane_head
            w_dt = w_in[:, dt_cols]
            pick = lambda f, b: jnp.where(jnp.asarray(lane_dir == 1), b[lane_head], f[lane_head]).reshape(1, V7X_LANES)
            z, xbc, pack_a, pack_b, pack_t = _ssd_in(
                h, s, w_in[:, :d_inner], w_in[:, d_inner:d_inner + conv_dim], ssd_conv_w[j], ssd_conv_b[j],
                w_dt, pick(ssd_dt_bias_fwd[j], ssd_dt_bias_bwd[j]), pick(ssd_a_log_fwd[j], ssd_a_log_bwd[j]))
            d_skip = jnp.repeat(ssd_d[j], HEAD_DIM).reshape(1, d_inner)
            yn = _ssd_scan(z, xbc, pack_a, pack_b, pack_t, d_skip, ssd_norm_g[j], bsz, s)
            h = _proj_ln(yn.reshape(n_rows, d_inner), h, ssd_w_out[j], ln_tok_g[i], ln_tok_b[i])
        h = _conv_ffn(h, s, ff_w_up[i], ff_conv_w[i], ff_conv_b[i], ff_w_down[i], ln_ffn_g[i], ln_ffn_b[i])
    return h.reshape(bsz, s, d)
```

```python
import functools
import math

import numpy as np
import jax
import jax.numpy as jnp
from jax import lax
from jax.experimental import pallas as pl
from jax.experimental.pallas import tpu as pltpu

F32 = jnp.float32
BF16 = jnp.bfloat16

LN_EPS = 1e-5
DEPTH = 2
N_MIXERS = 2
DEEPNORM_ALPHA = (2.0 * DEPTH) ** 0.25
FN_GROUPS = 8
HEAD_DIM = 64
D_STATE = 128
SSM_GROUPS = 8
SSM_EXPAND = 2
CHUNK = 128

V7X_LANES = 128
V7X_BF16_SUBLANES = 16
V7X_VMEM_LIMIT_BYTES = 56 * 1024 * 1024

TOKEN_TILE = 512
HALO = V7X_BF16_SUBLANES
FF_CHUNK = 256
XBC_CHUNK = 256
DFT_ROW_TILE = 256
PACK = 16
SCAN_UNROLL = (4, 8)


def _layer_norm(v, g, b):
    mu = jnp.mean(v, axis=-1, keepdims=True)
    vc = v - mu
    var = jnp.mean(vc * vc, axis=-1, keepdims=True)
    return vc * lax.rsqrt(var + LN_EPS) * g + b


def _silu(v):
    hv = 0.5 * v
    return hv + hv * jnp.tanh(hv)


def _dot(a, b):
    return jnp.dot(a, b, preferred_element_type=F32)


def _split_bf16(v, terms):
    out = []
    for _ in range(terms):
        t = v.astype(BF16)
        out.append(t)
        v = v - t.astype(F32)
    return out


def _params(*semantics):
    return pltpu.CompilerParams(dimension_semantics=semantics,
                                vmem_limit_bytes=V7X_VMEM_LIMIT_BYTES)


def _const_spec(shape):
    zeros = (0,) * len(shape)
    return pl.BlockSpec(shape, lambda *_: zeros)


def _fnet_in_kernel(x_ref, g_ref, b_ref, w_ref, bi_ref, cs_ref, h_ref, z_ref, *, groups):
    h = _layer_norm(x_ref[...], g_ref[...], b_ref[...])
    h_ref[...] = h
    u = (_dot(h.astype(BF16), w_ref[...]) + bi_ref[...]).astype(BF16)
    gd = u.shape[1] // groups
    for g in range(groups):
        r = _dot(u[:, g * gd:(g + 1) * gd], cs_ref[...])
        z_ref[0, :, g * gd:(g + 1) * gd] = r[:, :gd].astype(BF16)
        z_ref[1, :, g * gd:(g + 1) * gd] = r[:, gd:].astype(BF16)


def _fnet_in(x, ln_g, ln_b, w_in, b_in, cs):
    bsz, s, d = x.shape
    tm = min(TOKEN_TILE, s)
    nt = s // tm
    return pl.pallas_call(
        functools.partial(_fnet_in_kernel, groups=FN_GROUPS),
        grid=(bsz, nt),
        in_specs=[
            pl.BlockSpec((None, tm, d), lambda b, i: (b, i, 0)),
            _const_spec((1, d)), _const_spec((1, d)),
            _const_spec(w_in.shape), _const_spec((1, d)), _const_spec(cs.shape),
        ],
        out_specs=[
            pl.BlockSpec((None, tm, d), lambda b, i: (b, i, 0)),
            pl.BlockSpec((None, 2, tm, d), lambda b, i: (b, 0, i, 0)),
        ],
        out_shape=[jax.ShapeDtypeStruct((bsz, s, d), F32),
                   jax.ShapeDtypeStruct((bsz, 2, s, d), BF16)],
        compiler_params=_params("parallel", "parallel"),
        name="fnet_in",
    )(x, ln_g.reshape(1, d), ln_b.reshape(1, d), w_in, b_in.reshape(1, d), cs)


def _fnet_seq_kernel(t_ref, z_ref, h_ref, p_ref, w_ref, bo_ref, g_ref, b_ref, o_ref, y_ref, stash_ref,
                     *, n_half):
    i = pl.program_id(1)
    tr = y_ref.shape[0]
    s = z_ref.shape[0] // 2

    @pl.when(i < n_half)
    def _():
        a = _dot(t_ref[:, :s], z_ref[:s, :])
        b = _dot(t_ref[:, s:], z_ref[s:, :])
        y_ref[...] = (a - b)[:tr].astype(BF16)
        stash_ref[i] = (a + b).astype(BF16)

    @pl.when(i >= n_half)
    def _():
        y_ref[...] = _dot(p_ref[...], stash_ref[2 * n_half - 1 - i]).astype(BF16)

    mix = _dot(y_ref[...], w_ref[...]) + bo_ref[...]
    o_ref[...] = _layer_norm(DEEPNORM_ALPHA * h_ref[...] + mix, g_ref[...], b_ref[...])


def _fnet_seq(table, z, h, w_out, b_out, ln_g, ln_b):
    bsz, s, d = h.shape
    n_half, rows, _ = table.shape
    tr = rows - HALO
    z2 = z.reshape(bsz, 2 * s, d)
    flip = np.zeros((tr, rows), np.float32)
    flip[np.arange(tr), tr - np.arange(tr)] = 1.0
    return pl.pallas_call(
        functools.partial(_fnet_seq_kernel, n_half=n_half),
        grid=(bsz, 2 * n_half),
        in_specs=[
            pl.BlockSpec((None, rows, 2 * s), lambda b, i: (jnp.minimum(i, n_half - 1), 0, 0)),
            pl.BlockSpec((None, 2 * s, d), lambda b, i: (b, 0, 0), pipeline_mode=pl.Buffered(1)),
            pl.BlockSpec((None, tr, d), lambda b, i: (b, i, 0)),
            _const_spec(flip.shape), _const_spec(w_out.shape),
            _const_spec((1, d)), _const_spec((1, d)), _const_spec((1, d)),
        ],
        out_specs=pl.BlockSpec((None, tr, d), lambda b, i: (b, i, 0)),
        out_shape=jax.ShapeDtypeStruct((bsz, s, d), F32),
        scratch_shapes=[
            pltpu.VMEM((tr, d), BF16),
            pltpu.VMEM((n_half, rows, d), BF16),
        ],
        compiler_params=_params("parallel", "arbitrary"),
        name="fnet_seq",
    )(table, z2, h, jnp.asarray(flip, dtype=BF16), w_out, b_out.reshape(1, d),
      ln_g.reshape(1, d), ln_b.reshape(1, d))


def _assemble_halo(xb_ref, hp_ref, hm, hn_ref, tiles_per_seq):
    i = pl.program_id(0)
    tm = hm.shape[0]
    pos = lax.rem(i, tiles_per_seq)
    keep_prev = (pos != 0).astype(F32)
    keep_next = (pos != tiles_per_seq - 1).astype(F32)
    xb_ref[0:HALO, :] = (hp_ref[...] * keep_prev).astype(BF16)
    xb_ref[HALO:HALO + tm, :] = hm.astype(BF16)
    xb_ref[HALO + tm:, :] = (hn_ref[...] * keep_next).astype(BF16)


def _halo_specs(tm, d, n_rows):
    per = tm // HALO
    last = n_rows // HALO - 1
    return [
        pl.BlockSpec((HALO, d), lambda i: (jnp.maximum(i * per - 1, 0), 0)),
        pl.BlockSpec((tm, d), lambda i: (i, 0)),
        pl.BlockSpec((HALO, d), lambda i: (jnp.minimum((i + 1) * per, last), 0)),
    ]


def _conv_rows(u_ref, w, bias, tm, width):
    half = width // 2
    u = u_ref[...]
    n = u.shape[0]
    out = w[half:half + 1, :] * u[HALO:HALO + tm] + bias
    for k in range(width):
        if k != half:
            shifted = pltpu.roll(u, (half - k) % n, 0)[HALO:HALO + tm]
            out = out + w[k:k + 1, :] * shifted
    return out


def _conv_ffn_kernel(hp_ref, hm_ref, hn_ref, wg_ref, wv_ref, cwg_ref, cwv_ref, cbg_ref, cbv_ref,
                     wd_ref, g_ref, b_ref, o_ref, xb_ref, ug_ref, uv_ref, act_ref,
                     *, tiles_per_seq, n_chunks, width):
    hm = hm_ref[...]
    tm = hm.shape[0]
    cw = ug_ref.shape[2]
    _assemble_halo(xb_ref, hp_ref, hm, hn_ref, tiles_per_seq)
    for c in range(n_chunks):
        slot = c % 2
        xb = xb_ref[...]
        ug_ref[slot] = _dot(xb, wg_ref[c])
        uv_ref[slot] = _dot(xb, wv_ref[c])
        cg = _conv_rows(ug_ref.at[slot], cwg_ref[c], cbg_ref[c], tm, width)
        cv = _conv_rows(uv_ref.at[slot], cwv_ref[c], cbv_ref[c], tm, width)
        act_ref[:, c * cw:(c + 1) * cw] = (_silu(cg) * cv).astype(BF16)
    ffn = _dot(act_ref[...], wd_ref[...])
    o_ref[...] = _layer_norm(DEEPNORM_ALPHA * hm + ffn, g_ref[...], b_ref[...])


def _conv_ffn(h, seq_len, w_up, conv_w, conv_b, w_down, ln_g, ln_b):
    n_rows, d = h.shape
    d_ff = w_down.shape[0]
    width = conv_w.shape[0]
    tm = min(TOKEN_TILE, seq_len)
    cw = FF_CHUNK
    nc = d_ff // cw
    chunked = lambda a, rows: a.reshape(rows, nc, cw).transpose(1, 0, 2)
    wg = chunked(w_up[:, :d_ff], d).astype(BF16)
    wv = chunked(w_up[:, d_ff:], d).astype(BF16)
    cwg = chunked(conv_w[:, :d_ff], width)
    cwv = chunked(conv_w[:, d_ff:], width)
    cbg = conv_b[:d_ff].reshape(nc, 1, cw)
    cbv = conv_b[d_ff:].reshape(nc, 1, cw)
    wd = w_down.astype(BF16)
    return pl.pallas_call(
        functools.partial(_conv_ffn_kernel, tiles_per_seq=seq_len // tm, n_chunks=nc, width=width),
        grid=(n_rows // tm,),
        in_specs=_halo_specs(tm, d, n_rows) + [
            _const_spec(wg.shape), _const_spec(wv.shape), _const_spec(cwg.shape), _const_spec(cwv.shape),
            _const_spec(cbg.shape), _const_spec(cbv.shape), _const_spec(wd.shape),
            _const_spec((1, d)), _const_spec((1, d)),
        ],
        out_specs=pl.BlockSpec((tm, d), lambda i: (i, 0)),
        out_shape=jax.ShapeDtypeStruct((n_rows, d), F32),
        scratch_shapes=[
            pltpu.VMEM((tm + 2 * HALO, d), BF16),
            pltpu.VMEM((2, tm + 2 * HALO, cw), F32),
            pltpu.VMEM((2, tm + 2 * HALO, cw), F32),
            pltpu.VMEM((tm, d_ff), BF16),
        ],
        compiler_params=_params("parallel"),
        name="conv_ffn",
    )(h, h, h, wg, wv, cwg, cwv, cbg, cbv, wd, ln_g.reshape(1, d), ln_b.reshape(1, d))


def _ssd_in_kernel(hp_ref, hm_ref, hn_ref, wz_ref, wx_ref, cw_ref, cb_ref, wdt_ref, dtb_ref, alog_ref,
                   selc_ref, z_ref, xbc_ref, a3_ref, b2_ref, pt_ref, xb_ref, u_ref,
                   *, tiles_per_seq, n_chunks, width):
    hm = hm_ref[...]
    tm = hm.shape[0]
    cw = u_ref.shape[2]
    _assemble_halo(xb_ref, hp_ref, hm, hn_ref, tiles_per_seq)
    xm = xb_ref[HALO:HALO + tm, :]
    z_ref[...] = _dot(xm, wz_ref[...]).astype(BF16)

    for c in range(n_chunks):
        slot = c % 2
        u_ref[slot] = _dot(xb_ref[...], wx_ref[c])
        v = _conv_rows(u_ref.at[slot], cw_ref[c], cb_ref[c], tm, width)
        xbc_ref[:, c * cw:(c + 1) * cw] = _silu(v).astype(BF16)

    raw = _dot(xm, wdt_ref[...]) + dtb_ref[...]
    dt = jnp.maximum(raw, 0.0) + jnp.log(1.0 + jnp.exp(-jnp.abs(raw)))
    a = -jnp.exp(alog_ref[...])
    dta = dt * a
    lane = lax.broadcasted_iota(jnp.int32, (1, V7X_LANES), 1)
    is_cum = (lane & 8) != 0
    is_bwd = (lane & 4) != 0
    row = lax.broadcasted_iota(jnp.int32, (CHUNK, CHUNK), 0)
    col = lax.broadcasted_iota(jnp.int32, (CHUNK, CHUNK), 1)
    tri = (row >= col).astype(BF16)
    for k in range(tm // CHUNK):
        rows = slice(k * CHUNK, (k + 1) * CHUNK)
        dk = dta[rows, :]
        t0, t1, t2 = _split_bf16(dk, 3)
        cf = _dot(tri, t2) + _dot(tri, t1) + _dot(tri, t0)
        cb = cf[CHUNK - 1:CHUNK, :] - cf + dk
        cum = jnp.where(is_bwd, cb, cf)
        end = jnp.where(is_bwd, cum[0:1, :], cum[CHUNK - 1:CHUNK, :])
        dtk = dt[rows, :]
        pa = jnp.where(is_cum, cum, dtk)
        pb = jnp.where(is_cum, dtk * jnp.exp(end - cum), jnp.exp(cum))
        a3_ref[rows, :] = _dot(jnp.concatenate(_split_bf16(pa, 3), axis=1), selc_ref[...]).astype(BF16)
        for j, term in enumerate(_split_bf16(pb, 2)):
            b2_ref[rows, j * V7X_LANES:(j + 1) * V7X_LANES] = term
        pt_ref[:, rows] = pa.T


def _ssd_in(h, seq_len, w_z, w_xbc, conv_w, conv_b, w_dt, dt_bias, a_log):
    n_rows, d = h.shape
    d_inner = w_z.shape[1]
    conv_dim = w_xbc.shape[1]
    width = conv_w.shape[0]
    tm = min(TOKEN_TILE, seq_len)
    cw = XBC_CHUNK
    nc = conv_dim // cw
    wx = w_xbc.reshape(d, nc, cw).transpose(1, 0, 2).astype(BF16)
    cwc = conv_w.reshape(width, nc, cw).transpose(1, 0, 2)
    cbc = conv_b.reshape(nc, 1, cw)
    selc = jnp.asarray(_cumsum_compaction_matrix(), dtype=BF16)
    return pl.pallas_call(
        functools.partial(_ssd_in_kernel, tiles_per_seq=seq_len // tm, n_chunks=nc, width=width),
        grid=(n_rows // tm,),
        in_specs=_halo_specs(tm, d, n_rows) + [
            _const_spec(w_z.shape), _const_spec(wx.shape), _const_spec(cwc.shape), _const_spec(cbc.shape),
            _const_spec(w_dt.shape), _const_spec((1, V7X_LANES)), _const_spec((1, V7X_LANES)),
            _const_spec(selc.shape),
        ],
        out_specs=[
            pl.BlockSpec((tm, d_inner), lambda i: (i, 0)),
            pl.BlockSpec((tm, conv_dim), lambda i: (i, 0)),
            pl.BlockSpec((tm, 2 * V7X_LANES), lambda i: (i, 0)),
            pl.BlockSpec((tm, 2 * V7X_LANES), lambda i: (i, 0)),
            pl.BlockSpec((V7X_LANES, tm), lambda i: (0, i)),
        ],
        out_shape=[
            jax.ShapeDtypeStruct((n_rows, d_inner), BF16),
            jax.ShapeDtypeStruct((n_rows, conv_dim), BF16),
            jax.ShapeDtypeStruct((n_rows, 2 * V7X_LANES), BF16),
            jax.ShapeDtypeStruct((n_rows, 2 * V7X_LANES), BF16),
            jax.ShapeDtypeStruct((V7X_LANES, n_rows), F32),
        ],
        scratch_shapes=[
            pltpu.VMEM((tm + 2 * HALO, d), BF16),
            pltpu.VMEM((2, tm + 2 * HALO, cw), F32),
        ],
        compiler_params=_params("parallel"),
        name="ssd_in",
    )(h, h, h, w_z.astype(BF16), wx, cwc, cbc, w_dt.astype(BF16), dt_bias, a_log, selc)


def _ssd_scan_kernel(x_ref, b_ref, c_ref, z_ref, a3_ref, b2_ref, pt_ref, dsk_ref, ng_ref, o_ref,
                     y_ref, sb_ref, xbd_all_ref, sel_l_ref, sel_e_ref, sel_b_ref, *, n_chunks, heads, unroll):
    q = CHUNK
    p = HEAD_DIM
    w = heads * p
    g = pl.program_id(1)
    log_q, log_p = q.bit_length() - 1, p.bit_length() - 1

    src = lax.broadcasted_iota(jnp.int32, sel_l_ref.shape, 0)
    blk = lax.broadcasted_iota(jnp.int32, sel_l_ref.shape, 1) >> log_q
    hd = 4 * (blk & 1) + (blk >> 1)
    sel_l_ref[...] = (((src >> 5) == g) & ((src & 7) == hd) & (((src >> 3) & 3) != 3)).astype(BF16)

    def selection(ref, source_offset):
        src = lax.broadcasted_iota(jnp.int32, ref.shape, 0) & (V7X_LANES - 1)
        out = lax.broadcasted_iota(jnp.int32, ref.shape, 1)
        ref[...] = (src == PACK * g + source_offset(out)).astype(BF16)

    selection(sel_e_ref, lambda j: jnp.where(j < w, j >> log_p, 8 + ((j - w) >> log_p)))
    selection(sel_b_ref, lambda j: 4 + (j >> log_p))
    xbd_all_ref[...] = jnp.zeros_like(xbd_all_ref)

    row = lax.broadcasted_iota(jnp.int32, (q, q), 0)
    col = lax.broadcasted_iota(jnp.int32, (q, q), 1)
    lower = row >= col
    upper = row <= col
    neg_inf = jnp.float32(-jnp.inf)

    def pass1(i, hf):
        chunks = range(unroll[0])
        rows = [pl.ds(pl.multiple_of((i * unroll[0] + u) * q, q), q) for u in chunks]
        xc = [x_ref[rw, :] for rw in rows]
        bc = [b_ref[rw, :] for rw in rows]
        cc = [c_ref[rw, :] for rw in rows]
        pt = [pt_ref[:, rw] for rw in rows]
        a3 = [a3_ref[rw, :] for rw in rows]
        b2 = [b2_ref[rw, :] for rw in rows]
        cbm = [lax.dot_general(cc[u], bc[u], (((1,), (1,)), ((), ())), preferred_element_type=F32)
               for u in chunks]
        ms = [[] for _ in chunks]
        for r in range(heads):
            for u in chunks:
                cum_col = _dot(a3[u], sel_l_ref[:, 2 * r * q:2 * (r + 1) * q])
                seg_f = cum_col[:, :q] - pt[u][8 + r:9 + r, :]
                seg_b = cum_col[:, q:] - pt[u][12 + r:13 + r, :]
                lf = jnp.exp(jnp.where(lower, seg_f, neg_inf)) * pt[u][r:r + 1, :]
                lb = jnp.exp(jnp.where(upper, seg_b, neg_inf)) * pt[u][4 + r:5 + r, :]
                ms[u].append((cbm[u] * (lf + lb)).astype(BF16))
                xbd_all_ref[u, r * q:(r + 1) * q, r * p:(r + 1) * p] = xc[u][:, r * p:(r + 1) * p]
        yd = [_dot(jnp.concatenate(ms[u], axis=1), xbd_all_ref[u]) for u in chunks]
        xf = [xc[u].astype(F32) for u in chunks]
        xdec = [jnp.concatenate([(xf[u] * _dot(b2[u], sel_e_ref[:, w:2 * w])).astype(BF16),
                                 (xf[u] * _dot(b2[u], sel_e_ref[:, 2 * w:])).astype(BF16)], axis=1)
                for u in chunks]
        st = [lax.dot_general(bc[u], xdec[u], (((0,), (0,)), ((), ())), preferred_element_type=F32)
              for u in chunks]
        ef = [_dot(b2[u], sel_e_ref[:, :w]) for u in chunks]
        for u in chunks:
            y_ref[rows[u], :] = yd[u] + _dot(cc[u], hf.astype(BF16)) * ef[u] + dsk_ref[...] * xf[u]
            sb_ref[i * unroll[0] + u] = st[u][:, w:]
            hf = hf * ef[u][q - 1:q, :] + st[u][:, :w]
        return hf

    def pass2(k, hb):
        c = n_chunks - 1 - k
        rows = pl.ds(pl.multiple_of(c * q, q), q)
        eb = _dot(b2_ref[rows, :], sel_b_ref[...])
        y = y_ref[rows, :] + _dot(c_ref[rows, :], hb.astype(BF16)) * eb
        y = y * _silu(z_ref[rows, :].astype(F32))
        y = y * lax.rsqrt(jnp.mean(y * y, axis=-1, keepdims=True) + LN_EPS)
        o_ref[rows, :] = (y * ng_ref[...]).astype(BF16)
        return hb * eb[0:1, :] + sb_ref[c]

    zero = jnp.zeros((D_STATE, w), F32)
    lax.fori_loop(0, n_chunks // unroll[0], pass1, zero)
    lax.fori_loop(0, n_chunks, pass2, zero, unroll=unroll[1])


def _ssd_scan(z, xbc, a3, b2, pack_t, d_skip, norm_g, bsz, seq_len):
    d_inner = z.shape[1]
    groups = SSM_GROUPS
    w = d_inner // groups
    heads = w // HEAD_DIM
    n = D_STATE
    nc = seq_len // CHUNK
    assert nc % SCAN_UNROLL[0] == 0
    z3 = z.reshape(bsz, seq_len, d_inner)
    xbc3 = xbc.reshape(bsz, seq_len, xbc.shape[1])
    a33 = a3.reshape(bsz, seq_len, a3.shape[1])
    b23 = b2.reshape(bsz, seq_len, b2.shape[1])
    b_off = d_inner // n
    c_off = b_off + groups
    return pl.pallas_call(
        functools.partial(_ssd_scan_kernel, n_chunks=nc, heads=heads, unroll=SCAN_UNROLL),
        grid=(bsz, groups),
        in_specs=[
            pl.BlockSpec((None, seq_len, w), lambda b, g: (b, 0, g)),
            pl.BlockSpec((None, seq_len, n), lambda b, g: (b, 0, b_off + g)),
            pl.BlockSpec((None, seq_len, n), lambda b, g: (b, 0, c_off + g)),
            pl.BlockSpec((None, seq_len, w), lambda b, g: (b, 0, g)),
            pl.BlockSpec((None, seq_len, a3.shape[1]), lambda b, g: (b, 0, 0)),
            pl.BlockSpec((None, seq_len, b2.shape[1]), lambda b, g: (b, 0, 0)),
            pl.BlockSpec((PACK, seq_len), lambda b, g: (g, b)),
            pl.BlockSpec((1, w), lambda b, g: (0, g)),
            pl.BlockSpec((1, w), lambda b, g: (0, g)),
        ],
        out_specs=pl.BlockSpec((None, seq_len, w), lambda b, g: (b, 0, g)),
        out_shape=jax.ShapeDtypeStruct((bsz, seq_len, d_inner), BF16),
        scratch_shapes=[
            pltpu.VMEM((seq_len, w), F32),
            pltpu.VMEM((nc, n, w), F32),
            pltpu.VMEM((SCAN_UNROLL[0], heads * CHUNK, w), BF16),
            pltpu.VMEM((a3.shape[1], 2 * heads * CHUNK), BF16),
            pltpu.VMEM((b2.shape[1], 3 * w), BF16),
            pltpu.VMEM((b2.shape[1], w), BF16),
        ],
        compiler_params=_params("parallel", "arbitrary"),
        name="ssd_scan",
    )(xbc3, xbc3, xbc3, z3, a33, b23, pack_t, d_skip, norm_g.reshape(1, d_inner))


def _proj_ln_kernel(y_ref, h_ref, w_ref, g_ref, b_ref, o_ref):
    mix = _dot(y_ref[...], w_ref[...])
    o_ref[...] = _layer_norm(DEEPNORM_ALPHA * h_ref[...] + mix, g_ref[...], b_ref[...])


def _proj_ln(y, h, w, ln_g, ln_b):
    n_rows, d = h.shape
    k = y.shape[1]
    tm = min(TOKEN_TILE, n_rows)
    return pl.pallas_call(
        _proj_ln_kernel,
        grid=(n_rows // tm,),
        in_specs=[
            pl.BlockSpec((tm, k), lambda i: (i, 0)),
            pl.BlockSpec((tm, d), lambda i: (i, 0)),
            _const_spec(w.shape), _const_spec((1, d)), _const_spec((1, d)),
        ],
        out_specs=pl.BlockSpec((tm, d), lambda i: (i, 0)),
        out_shape=jax.ShapeDtypeStruct((n_rows, d), F32),
        compiler_params=_params("parallel"),
        name="proj_ln",
    )(y, h, w.astype(BF16), ln_g.reshape(1, d), ln_b.reshape(1, d))


def _channel_dft_table(gd):
    k = np.arange(gd)
    ang = 2.0 * np.pi * ((k[:, None] * k[None, :]) % gd) / gd
    t = np.concatenate([np.cos(ang), np.sin(ang)], axis=1) / math.sqrt(gd)
    return jnp.asarray(t, dtype=BF16)


def _seq_dft_table(s):
    tr = min(DFT_ROW_TILE, s // 2)
    n_half = s // 2 // tr
    n = jnp.arange(s, dtype=jnp.int32)
    row = jnp.arange(tr + HALO, dtype=jnp.int32)
    tile = jnp.arange(n_half, dtype=jnp.int32) * tr
    unit = 2.0 * math.pi / s
    ang_r = unit * ((row[:, None] * n[None, :]) % s).astype(F32)
    ang_t = unit * ((tile[:, None] * n[None, :]) % s).astype(F32)
    scale = 1.0 / math.sqrt(s)
    cr, sr = jnp.cos(ang_r)[None] * scale, jnp.sin(ang_r)[None] * scale
    ct, st = jnp.cos(ang_t)[:, None], jnp.sin(ang_t)[:, None]
    return jnp.concatenate([ct * cr - st * sr, st * cr + ct * sr], axis=2).astype(BF16)


def _cumsum_compaction_matrix():
    m = np.zeros((3 * V7X_LANES, 2 * V7X_LANES), np.float32)
    for g in range(SSM_GROUPS):
        for t in range(3):
            for hd in range(8):
                m[t * V7X_LANES + PACK * g + 8 + hd, 32 * g + 8 * t + hd] = 1.0
    return m


def _packed_lane_heads():
    lane = np.arange(V7X_LANES)
    g, d, r = lane // PACK, (lane // 4) % 2, lane % 4
    return d, g * 4 + r


def kernel(x, emb_ln_g, emb_ln_b, fn_w_in, fn_b_in, fn_w_out, fn_b_out, ssd_w_in, ssd_conv_w, ssd_conv_b, ssd_a_log_fwd, ssd_a_log_bwd, ssd_dt_bias_fwd, ssd_dt_bias_bwd, ssd_d, ssd_norm_g, ssd_w_out, ln_tok_g, ln_tok_b, ff_w_up, ff_conv_w, ff_conv_b, ff_w_down, ln_ffn_g, ln_ffn_b):
    bsz, s, d = x.shape
    n_rows = bsz * s
    d_inner = SSM_EXPAND * d
    n_heads = d_inner // HEAD_DIM
    conv_dim = d_inner + 2 * SSM_GROUPS * D_STATE
    assert s % TOKEN_TILE == 0 and s % CHUNK == 0 and d % (FN_GROUPS * V7X_LANES) == 0
    assert n_heads // SSM_GROUPS == 4 and SSM_GROUPS * PACK == V7X_LANES

    h = None
    for i in range(DEPTH):
        j = i // N_MIXERS
        if i % N_MIXERS == 0:
            cs = _channel_dft_table(d // FN_GROUPS)
            table = _seq_dft_table(s)
            if h is None:
                h0, z = _fnet_in(x, emb_ln_g, emb_ln_b, fn_w_in[j].astype(BF16), fn_b_in[j], cs)
            else:
                raise NotImplementedError("only the first layer uses the Fourier mixer at this depth")
            h = _fnet_seq(table, z, h0, fn_w_out[j].astype(BF16), fn_b_out[j],
                          ln_tok_g[i], ln_tok_b[i]).reshape(n_rows, d)
        else:
            w_in = ssd_w_in[j]
            lane_dir, lane_head = _packed_lane_heads()
            dt_cols = d_inner + conv_dim + lane_dir * n_heads + lane_head
            w_dt = w_in[:, dt_cols]
            pick = lambda f, b: jnp.where(jnp.asarray(lane_dir == 1), b[lane_head], f[lane_head]).reshape(1, V7X_LANES)
            z, xbc, a3, b2, pack_t = _ssd_in(
                h, s, w_in[:, :d_inner], w_in[:, d_inner:d_inner + conv_dim], ssd_conv_w[j], ssd_conv_b[j],
                w_dt, pick(ssd_dt_bias_fwd[j], ssd_dt_bias_bwd[j]), pick(ssd_a_log_fwd[j], ssd_a_log_bwd[j]))
            d_skip = jnp.repeat(ssd_d[j], HEAD_DIM).reshape(1, d_inner)
            yn = _ssd_scan(z, xbc, a3, b2, pack_t, d_skip, ssd_norm_g[j], bsz, s)
            h = _proj_ln(yn.reshape(n_rows, d_inner), h, ssd_w_out[j], ln_tok_g[i], ln_tok_b[i])
        h = _conv_ffn(h, s, ff_w_up[i], ff_conv_w[i], ff_conv_b[i], ff_w_down[i], ln_ffn_g[i], ln_ffn_b[i])
    return h.reshape(bsz, s, d)
```

```python
import functools
import math

import numpy as np
import jax
import jax.numpy as jnp
from jax import lax
from jax.experimental import pallas as pl
from jax.experimental.pallas import tpu as pltpu

F32 = jnp.float32
BF16 = jnp.bfloat16

LN_EPS = 1e-5
DEPTH = 2
N_MIXERS = 2
DEEPNORM_ALPHA = (2.0 * DEPTH) ** 0.25
FN_GROUPS = 8
HEAD_DIM = 64
D_STATE = 128
SSM_GROUPS = 8
SSM_EXPAND = 2
CHUNK = 128

V7X_LANES = 128
V7X_BF16_SUBLANES = 16
V7X_VMEM_LIMIT_BYTES = 56 * 1024 * 1024

TOKEN_TILE = 512
HALO = V7X_BF16_SUBLANES
FF_CHUNK = 256
XBC_CHUNK = 256
DFT_ROW_TILE = 256
PACK = 16
PROJ_AHEAD = 2
SCAN_UNROLL = (8, 8)


def _layer_norm(v, g, b):
    mu = jnp.mean(v, axis=-1, keepdims=True)
    vc = v - mu
    var = jnp.mean(vc * vc, axis=-1, keepdims=True)
    return vc * lax.rsqrt(var + LN_EPS) * g + b


def _silu(v):
    hv = 0.5 * v
    return hv + hv * jnp.tanh(hv)


def _dot(a, b):
    return jnp.dot(a, b, preferred_element_type=F32)


def _split_bf16(v, terms):
    out = []
    for _ in range(terms):
        t = v.astype(BF16)
        out.append(t)
        v = v - t.astype(F32)
    return out


def _params(*semantics):
    return pltpu.CompilerParams(dimension_semantics=semantics,
                                vmem_limit_bytes=V7X_VMEM_LIMIT_BYTES)


def _const_spec(shape):
    zeros = (0,) * len(shape)
    return pl.BlockSpec(shape, lambda *_: zeros, pipeline_mode=pl.Buffered(1))


def _fnet_in_kernel(x_ref, g_ref, b_ref, w_ref, bi_ref, cs_ref, h_ref, z_ref, *, groups):
    h = _layer_norm(x_ref[...], g_ref[...], b_ref[...])
    h_ref[...] = h
    u = (_dot(h.astype(BF16), w_ref[...]) + bi_ref[...]).astype(BF16)
    gd = u.shape[1] // groups
    for g in range(groups):
        r = _dot(u[:, g * gd:(g + 1) * gd], cs_ref[...])
        z_ref[0, :, g * gd:(g + 1) * gd] = r[:, :gd].astype(BF16)
        z_ref[1, :, g * gd:(g + 1) * gd] = r[:, gd:].astype(BF16)


def _fnet_in(x, ln_g, ln_b, w_in, b_in, cs):
    bsz, s, d = x.shape
    tm = min(TOKEN_TILE, s)
    nt = s // tm
    return pl.pallas_call(
        functools.partial(_fnet_in_kernel, groups=FN_GROUPS),
        grid=(bsz, nt),
        in_specs=[
            pl.BlockSpec((None, tm, d), lambda b, i: (b, i, 0)),
            _const_spec((1, d)), _const_spec((1, d)),
            _const_spec(w_in.shape), _const_spec((1, d)), _const_spec(cs.shape),
        ],
        out_specs=[
            pl.BlockSpec((None, tm, d), lambda b, i: (b, i, 0)),
            pl.BlockSpec((None, 2, tm, d), lambda b, i: (b, 0, i, 0)),
        ],
        out_shape=[jax.ShapeDtypeStruct((bsz, s, d), F32),
                   jax.ShapeDtypeStruct((bsz, 2, s, d), BF16)],
        compiler_params=_params("parallel", "parallel"),
        name="fnet_in",
    )(x, ln_g.reshape(1, d), ln_b.reshape(1, d), w_in, b_in.reshape(1, d), cs)


def _fnet_seq_kernel(t_ref, z_ref, h_ref, p_ref, w_ref, bo_ref, g_ref, b_ref, o_ref, y_ref, stash_ref,
                     *, n_half):
    i = pl.program_id(1)
    tr = y_ref.shape[0]
    s = z_ref.shape[0] // 2

    @pl.when(i < n_half)
    def _():
        a = _dot(t_ref[:, :s], z_ref[:s, :])
        b = _dot(t_ref[:, s:], z_ref[s:, :])
        y_ref[...] = (a - b)[:tr].astype(BF16)
        stash_ref[i] = (a + b).astype(BF16)

    @pl.when(i >= n_half)
    def _():
        y_ref[...] = _dot(p_ref[...], stash_ref[2 * n_half - 1 - i]).astype(BF16)

    mix = _dot(y_ref[...], w_ref[...]) + bo_ref[...]
    o_ref[...] = _layer_norm(DEEPNORM_ALPHA * h_ref[...] + mix, g_ref[...], b_ref[...])


def _fnet_seq(table, z, h, w_out, b_out, ln_g, ln_b):
    bsz, s, d = h.shape
    n_half, rows, _ = table.shape
    tr = rows - HALO
    z2 = z.reshape(bsz, 2 * s, d)
    flip = np.zeros((tr, rows), np.float32)
    flip[np.arange(tr), tr - np.arange(tr)] = 1.0
    return pl.pallas_call(
        functools.partial(_fnet_seq_kernel, n_half=n_half),
        grid=(bsz, 2 * n_half),
        in_specs=[
            pl.BlockSpec((None, rows, 2 * s), lambda b, i: (jnp.minimum(i, n_half - 1), 0, 0)),
            pl.BlockSpec((None, 2 * s, d), lambda b, i: (b, 0, 0), pipeline_mode=pl.Buffered(1)),
            pl.BlockSpec((None, tr, d), lambda b, i: (b, i, 0)),
            _const_spec(flip.shape), _const_spec(w_out.shape),
            _const_spec((1, d)), _const_spec((1, d)), _const_spec((1, d)),
        ],
        out_specs=pl.BlockSpec((None, tr, d), lambda b, i: (b, i, 0)),
        out_shape=jax.ShapeDtypeStruct((bsz, s, d), F32),
        scratch_shapes=[
            pltpu.VMEM((tr, d), BF16),
            pltpu.VMEM((n_half, rows, d), BF16),
        ],
        compiler_params=_params("parallel", "arbitrary"),
        name="fnet_seq",
    )(table, z2, h, jnp.asarray(flip, dtype=BF16), w_out, b_out.reshape(1, d),
      ln_g.reshape(1, d), ln_b.reshape(1, d))


def _halo_specs(tm, d, n_rows):
    per = tm // HALO
    last = n_rows // HALO - 1
    return [
        pl.BlockSpec((HALO, d), lambda i: (jnp.maximum(i * per - 1, 0), 0)),
        pl.BlockSpec((tm, d), lambda i: (i, 0)),
        pl.BlockSpec((HALO, d), lambda i: (jnp.minimum((i + 1) * per, last), 0)),
    ]


def _assemble_interleaved(xp_ref, nat_ref, hp_ref, hm_ref, hn_ref, tiles_per_seq):
    tm, d = hm_ref.shape
    r = tm + 2 * HALO
    groups = r // 8
    pos = lax.rem(pl.program_id(0), tiles_per_seq)
    keep_prev = (pos != 0).astype(F32)
    keep_next = (pos != tiles_per_seq - 1).astype(F32)
    lane_tiles = d // V7X_LANES
    for l in range(lane_tiles):
        lanes = slice(l * V7X_LANES, (l + 1) * V7X_LANES)
        nat_ref[l, 0:HALO, :] = hp_ref[:, lanes] * keep_prev
        nat_ref[l, HALO:HALO + tm, :] = hm_ref[:, lanes]
        nat_ref[l, HALO + tm:, :] = hn_ref[:, lanes] * keep_next
    gather = lambda j: jnp.concatenate(
        [nat_ref[l, pl.ds(j, 8, stride=groups), :] for l in range(lane_tiles)], axis=1)
    for j in range(0, groups, 2):
        xp_ref[8 * j:8 * j + 16, :] = jnp.concatenate([gather(j), gather(j + 1)], axis=0).astype(BF16)


def _natural_rows(val, un_ref):
    r, c = val.shape
    groups = r // 8
    lane_tiles = c // V7X_LANES
    for j in range(groups):
        for l in range(lane_tiles):
            un_ref[l, pl.ds(j, 8, stride=groups), :] = val[8 * j:8 * j + 8, l * V7X_LANES:(l + 1) * V7X_LANES]
    return jnp.concatenate([un_ref[l, HALO:r - HALO, :] for l in range(lane_tiles)], axis=1)


def _tap(u, delta):
    r = u.shape[0]
    if delta == 0:
        return u
    if delta > 0:
        wrap = pltpu.roll(u[:8 * delta], 8 * delta - 1, 0)
        return jnp.concatenate([u[8 * delta:], wrap], axis=0)
    wrap = pltpu.roll(u[r + 8 * delta:], 1, 0)
    return jnp.concatenate([wrap, u[:r + 8 * delta]], axis=0)


def _conv_taps(u, w, bias):
    width = w.shape[0]
    half = width // 2
    out = w[half:half + 1, :] * u + bias
    for k in range(width):
        if k != half:
            out = out + w[k:k + 1, :] * _tap(u, k - half)
    return out


def _conv_ffn_kernel(hp_ref, hm_ref, hn_ref, wu_ref, cw_ref, cb_ref, wd_ref, g_ref, b_ref, o_ref,
                     xp_ref, nat_ref, ug_ref, uv_ref, act_ref, *, tiles_per_seq, n_chunks):
    cw = ug_ref.shape[2]
    d_ff = n_chunks * cw
    slots = ug_ref.shape[0]
    _assemble_interleaved(xp_ref, nat_ref, hp_ref, hm_ref, hn_ref, tiles_per_seq)

    def project(c):
        gate, val = slice(c * cw, (c + 1) * cw), slice(d_ff + c * cw, d_ff + (c + 1) * cw)
        ug_ref[c % slots] = _dot(xp_ref[...], wu_ref[:, gate])
        uv_ref[c % slots] = _dot(xp_ref[...], wu_ref[:, val])

    for c in range(min(PROJ_AHEAD, n_chunks)):
        project(c)
    for c in range(n_chunks):
        if c + PROJ_AHEAD < n_chunks:
            project(c + PROJ_AHEAD)
        gate, val = slice(c * cw, (c + 1) * cw), slice(d_ff + c * cw, d_ff + (c + 1) * cw)
        cg = _conv_taps(ug_ref[c % slots], cw_ref[:, gate], cb_ref[:, gate])
        cv = _conv_taps(uv_ref[c % slots], cw_ref[:, val], cb_ref[:, val])
        act_ref[:, gate] = (_silu(cg) * cv).astype(BF16)
    ffn = _natural_rows(_dot(act_ref[...], wd_ref[...]), nat_ref)
    o_ref[...] = _layer_norm(DEEPNORM_ALPHA * hm_ref[...] + ffn, g_ref[...], b_ref[...])


def _conv_ffn(h, seq_len, w_up, conv_w, conv_b, w_down, ln_g, ln_b):
    n_rows, d = h.shape
    d_ff = w_down.shape[0]
    tm = min(TOKEN_TILE, seq_len)
    r = tm + 2 * HALO
    cw = FF_CHUNK
    nc = d_ff // cw
    assert d_ff % cw == 0 and conv_w.shape[0] // 2 <= HALO
    wu = w_up.astype(BF16)
    wd = w_down.astype(BF16)
    return pl.pallas_call(
        functools.partial(_conv_ffn_kernel, tiles_per_seq=seq_len // tm, n_chunks=nc),
        grid=(n_rows // tm,),
        in_specs=_halo_specs(tm, d, n_rows) + [
            _const_spec(wu.shape), _const_spec(conv_w.shape), _const_spec((1, 2 * d_ff)),
            _const_spec(wd.shape), _const_spec((1, d)), _const_spec((1, d)),
        ],
        out_specs=pl.BlockSpec((tm, d), lambda i: (i, 0)),
        out_shape=jax.ShapeDtypeStruct((n_rows, d), F32),
        scratch_shapes=[
            pltpu.VMEM((r, d), BF16),
            pltpu.VMEM((d // V7X_LANES, r, V7X_LANES), F32),
            pltpu.VMEM((PROJ_AHEAD + 1, r, cw), F32),
            pltpu.VMEM((PROJ_AHEAD + 1, r, cw), F32),
            pltpu.VMEM((r, d_ff), BF16),
        ],
        compiler_params=_params("parallel"),
        name="conv_ffn",
    )(h, h, h, wu, conv_w, conv_b.reshape(1, 2 * d_ff), wd, ln_g.reshape(1, d), ln_b.reshape(1, d))


def _ssd_in_kernel(hp_ref, hm_ref, hn_ref, win_ref, cw_ref, cb_ref, wdt_ref, dtb_ref, alog_ref,
                   selc_ref, z_ref, xbc_ref, a3_ref, b2_ref, pt_ref, xp_ref, nat_ref, xn_ref, u_ref, un_ref,
                   *, tiles_per_seq, n_chunks):
    tm = hm_ref.shape[0]
    cw = u_ref.shape[2]
    slots = u_ref.shape[0]
    d_inner = z_ref.shape[1]
    _assemble_interleaved(xp_ref, nat_ref, hp_ref, hm_ref, hn_ref, tiles_per_seq)
    xn_ref[...] = hm_ref[...].astype(BF16)

    lane = lax.broadcasted_iota(jnp.int32, (1, V7X_LANES), 1)
    is_cum = (lane & 8) != 0
    is_bwd = (lane & 4) != 0
    row = lax.broadcasted_iota(jnp.int32, (CHUNK, CHUNK), 0)
    col = lax.broadcasted_iota(jnp.int32, (CHUNK, CHUNK), 1)
    tri = (row >= col).astype(BF16)
    n_sub = tm // CHUNK
    state = {}

    def dt_project():
        raw = _dot(xn_ref[...], wdt_ref[...]) + dtb_ref[...]
        state["dt"] = jnp.maximum(raw, 0.0) + jnp.log(1.0 + jnp.exp(-jnp.abs(raw)))
        state["dta"] = state["dt"] * -jnp.exp(alog_ref[...])

    def dt_cumsum(k):
        dk = state["dta"][k * CHUNK:(k + 1) * CHUNK, :]
        t0, t1, t2 = _split_bf16(dk, 3)
        state["cf", k] = _dot(tri, t2) + _dot(tri, t1) + _dot(tri, t0)

    def dt_pack(k):
        rows = slice(k * CHUNK, (k + 1) * CHUNK)
        dk, cf, dtk = state["dta"][rows, :], state["cf", k], state["dt"][rows, :]
        cb = cf[CHUNK - 1:CHUNK, :] - cf + dk
        cum = jnp.where(is_bwd, cb, cf)
        end = jnp.where(is_bwd, cum[0:1, :], cum[CHUNK - 1:CHUNK, :])
        pa = jnp.where(is_cum, cum, dtk)
        pb = jnp.where(is_cum, dtk * jnp.exp(end - cum), jnp.exp(cum))
        a3_ref[rows, :] = _dot(jnp.concatenate(_split_bf16(pa, 3), axis=1), selc_ref[...]).astype(BF16)
        for j, term in enumerate(_split_bf16(pb, 2)):
            b2_ref[rows, j * V7X_LANES:(j + 1) * V7X_LANES] = term
        pt_ref[:, rows] = pa.T

    stages = ([dt_project] + [functools.partial(dt_cumsum, k) for k in range(n_sub)]
              + [functools.partial(dt_pack, k) for k in range(n_sub)])

    def project(c):
        u_ref[c % slots] = _dot(xp_ref[...], win_ref[:, d_inner + c * cw:d_inner + (c + 1) * cw])

    z_steps = d_inner // cw
    for c in range(min(PROJ_AHEAD, n_chunks)):
        project(c)
    for c in range(n_chunks):
        if c + PROJ_AHEAD < n_chunks:
            project(c + PROJ_AHEAD)
        if c < len(stages):
            stages[c]()
        for s in range(c * z_steps // n_chunks, (c + 1) * z_steps // n_chunks):
            zc = slice(s * cw, (s + 1) * cw)
            z_ref[:, zc] = _dot(xn_ref[...], win_ref[:, zc]).astype(BF16)
        cols = slice(c * cw, (c + 1) * cw)
        v = _silu(_conv_taps(u_ref[c % slots], cw_ref[:, cols], cb_ref[:, cols]))
        xbc_ref[:, cols] = _natural_rows(v, un_ref.at[c % 2]).astype(BF16)
    for stage in stages[n_chunks:]:
        stage()


def _ssd_in(h, seq_len, w_in, d_inner, conv_w, conv_b, w_dt, dt_bias, a_log):
    n_rows, d = h.shape
    width, conv_dim = conv_w.shape
    tm = min(TOKEN_TILE, seq_len)
    r = tm + 2 * HALO
    cw = XBC_CHUNK
    nc = conv_dim // cw
    assert conv_dim % cw == 0 and d_inner % V7X_LANES == 0 and width // 2 <= HALO
    selc = jnp.asarray(_cumsum_compaction_matrix(), dtype=BF16)
    return pl.pallas_call(
        functools.partial(_ssd_in_kernel, tiles_per_seq=seq_len // tm, n_chunks=nc),
        grid=(n_rows // tm,),
        in_specs=_halo_specs(tm, d, n_rows) + [
            _const_spec(w_in.shape), _const_spec(conv_w.shape), _const_spec((1, conv_dim)),
            _const_spec(w_dt.shape), _const_spec((1, V7X_LANES)), _const_spec((1, V7X_LANES)),
            _const_spec(selc.shape),
        ],
        out_specs=[
            pl.BlockSpec((tm, d_inner), lambda i: (i, 0)),
            pl.BlockSpec((tm, conv_dim), lambda i: (i, 0)),
            pl.BlockSpec((tm, 2 * V7X_LANES), lambda i: (i, 0)),
            pl.BlockSpec((tm, 2 * V7X_LANES), lambda i: (i, 0)),
            pl.BlockSpec((V7X_LANES, tm), lambda i: (0, i)),
        ],
        out_shape=[
            jax.ShapeDtypeStruct((n_rows, d_inner), BF16),
            jax.ShapeDtypeStruct((n_rows, conv_dim), BF16),
            jax.ShapeDtypeStruct((n_rows, 2 * V7X_LANES), BF16),
            jax.ShapeDtypeStruct((n_rows, 2 * V7X_LANES), BF16),
            jax.ShapeDtypeStruct((V7X_LANES, n_rows), F32),
        ],
        scratch_shapes=[
            pltpu.VMEM((r, d), BF16),
            pltpu.VMEM((d // V7X_LANES, r, V7X_LANES), F32),
            pltpu.VMEM((tm, d), BF16),
            pltpu.VMEM((PROJ_AHEAD + 1, r, cw), F32),
            pltpu.VMEM((2, cw // V7X_LANES, r, V7X_LANES), F32),
        ],
        compiler_params=_params("parallel"),
        name="ssd_in",
    )(h, h, h, w_in, conv_w, conv_b.reshape(1, conv_dim), w_dt, dt_bias, a_log, selc)


def _ssd_scan_kernel(x_ref, b_ref, c_ref, z_ref, a3_ref, b2_ref, pt_ref, dsk_ref, ng_ref, o_ref,
                     y_ref, sb_ref, xbd_all_ref, sel_l_ref, sel_e_ref, sel_b_ref, *, n_chunks, heads, unroll):
    q = CHUNK
    p = HEAD_DIM
    w = heads * p
    g = pl.program_id(1)
    log_q, log_p = q.bit_length() - 1, p.bit_length() - 1

    src = lax.broadcasted_iota(jnp.int32, sel_l_ref.shape, 0)
    blk = lax.broadcasted_iota(jnp.int32, sel_l_ref.shape, 1) >> log_q
    hd = 4 * (blk & 1) + (blk >> 1)
    sel_l_ref[...] = (((src >> 5) == g) & ((src & 7) == hd) & (((src >> 3) & 3) != 3)).astype(BF16)

    def selection(ref, source_offset):
        src = lax.broadcasted_iota(jnp.int32, ref.shape, 0) & (V7X_LANES - 1)
        out = lax.broadcasted_iota(jnp.int32, ref.shape, 1)
        ref[...] = (src == PACK * g + source_offset(out)).astype(BF16)

    selection(sel_e_ref, lambda j: jnp.where(j < w, j >> log_p, 8 + ((j - w) >> log_p)))
    selection(sel_b_ref, lambda j: 4 + (j >> log_p))
    xbd_all_ref[...] = jnp.zeros_like(xbd_all_ref)

    row = lax.broadcasted_iota(jnp.int32, (q, q), 0)
    col = lax.broadcasted_iota(jnp.int32, (q, q), 1)
    lower = row >= col
    upper = row <= col
    neg_inf = jnp.float32(-jnp.inf)

    def pass1(i, hf):
        chunks = range(unroll[0])
        rows = [pl.ds(pl.multiple_of((i * unroll[0] + u) * q, q), q) for u in chunks]
        xc = [x_ref[rw, :] for rw in rows]
        bc = [b_ref[rw, :] for rw in rows]
        cc = [c_ref[rw, :] for rw in rows]
        pt = [pt_ref[:, rw] for rw in rows]
        a3 = [a3_ref[rw, :] for rw in rows]
        b2 = [b2_ref[rw, :] for rw in rows]
        cbm = [lax.dot_general(cc[u], bc[u], (((1,), (1,)), ((), ())), preferred_element_type=F32)
               for u in chunks]
        ms = [[] for _ in chunks]
        for r in range(heads):
            for u in chunks:
                cum_col = _dot(a3[u], sel_l_ref[:, 2 * r * q:2 * (r + 1) * q])
                seg_f = cum_col[:, :q] - pt[u][8 + r:9 + r, :]
                seg_b = cum_col[:, q:] - pt[u][12 + r:13 + r, :]
                lf = jnp.exp(jnp.where(lower, seg_f, neg_inf)) * pt[u][r:r + 1, :]
                lb = jnp.exp(jnp.where(upper, seg_b, neg_inf)) * pt[u][4 + r:5 + r, :]
                ms[u].append((cbm[u] * (lf + lb)).astype(BF16))
                xbd_all_ref[u, r * q:(r + 1) * q, r * p:(r + 1) * p] = xc[u][:, r * p:(r + 1) * p]
        yd = [_dot(jnp.concatenate(ms[u], axis=1), xbd_all_ref[u]) for u in chunks]
        xf = [xc[u].astype(F32) for u in chunks]
        xdec = [jnp.concatenate([(xf[u] * _dot(b2[u], sel_e_ref[:, w:2 * w])).astype(BF16),
                                 (xf[u] * _dot(b2[u], sel_e_ref[:, 2 * w:])).astype(BF16)], axis=1)
                for u in chunks]
        st = [lax.dot_general(bc[u], xdec[u], (((0,), (0,)), ((), ())), preferred_element_type=F32)
              for u in chunks]
        ef = [_dot(b2[u], sel_e_ref[:, :w]) for u in chunks]
        for u in chunks:
            y_ref[rows[u], :] = yd[u] + _dot(cc[u], hf.astype(BF16)) * ef[u] + dsk_ref[...] * xf[u]
            sb_ref[i * unroll[0] + u] = st[u][:, w:]
            hf = hf * ef[u][q - 1:q, :] + st[u][:, :w]
        return hf

    def pass2(k, hb):
        c = n_chunks - 1 - k
        rows = pl.ds(pl.multiple_of(c * q, q), q)
        eb = _dot(b2_ref[rows, :], sel_b_ref[...])
        y = y_ref[rows, :] + _dot(c_ref[rows, :], hb.astype(BF16)) * eb
        y = y * _silu(z_ref[rows, :].astype(F32))
        y = y * lax.rsqrt(jnp.mean(y * y, axis=-1, keepdims=True) + LN_EPS)
        o_ref[rows, :] = (y * ng_ref[...]).astype(BF16)
        return hb * eb[0:1, :] + sb_ref[c]

    zero = jnp.zeros((D_STATE, w), F32)
    lax.fori_loop(0, n_chunks // unroll[0], pass1, zero)
    lax.fori_loop(0, n_chunks, pass2, zero, unroll=unroll[1])


def _ssd_scan(z, xbc, a3, b2, pack_t, d_skip, norm_g, bsz, seq_len):
    d_inner = z.shape[1]
    groups = SSM_GROUPS
    w = d_inner // groups
    heads = w // HEAD_DIM
    n = D_STATE
    nc = seq_len // CHUNK
    assert nc % SCAN_UNROLL[0] == 0
    z3 = z.reshape(bsz, seq_len, d_inner)
    xbc3 = xbc.reshape(bsz, seq_len, xbc.shape[1])
    a33 = a3.reshape(bsz, seq_len, a3.shape[1])
    b23 = b2.reshape(bsz, seq_len, b2.shape[1])
    b_off = d_inner // n
    c_off = b_off + groups
    return pl.pallas_call(
        functools.partial(_ssd_scan_kernel, n_chunks=nc, heads=heads, unroll=SCAN_UNROLL),
        grid=(bsz, groups),
        in_specs=[
            pl.BlockSpec((None, seq_len, w), lambda b, g: (b, 0, g)),
            pl.BlockSpec((None, seq_len, n), lambda b, g: (b, 0, b_off + g)),
            pl.BlockSpec((None, seq_len, n), lambda b, g: (b, 0, c_off + g)),
            pl.BlockSpec((None, seq_len, w), lambda b, g: (b, 0, g)),
            pl.BlockSpec((None, seq_len, a3.shape[1]), lambda b, g: (b, 0, 0)),
            pl.BlockSpec((None, seq_len, b2.shape[1]), lambda b, g: (b, 0, 0)),
            pl.BlockSpec((PACK, seq_len), lambda b, g: (g, b)),
            pl.BlockSpec((1, w), lambda b, g: (0, g)),
            pl.BlockSpec((1, w), lambda b, g: (0, g)),
        ],
        out_specs=pl.BlockSpec((None, seq_len, w), lambda b, g: (b, 0, g)),
        out_shape=jax.ShapeDtypeStruct((bsz, seq_len, d_inner), BF16),
        scratch_shapes=[
            pltpu.VMEM((seq_len, w), F32),
            pltpu.VMEM((nc, n, w), F32),
            pltpu.VMEM((SCAN_UNROLL[0], heads * CHUNK, w), BF16),
            pltpu.VMEM((a3.shape[1], 2 * heads * CHUNK), BF16),
            pltpu.VMEM((b2.shape[1], 3 * w), BF16),
            pltpu.VMEM((b2.shape[1], w), BF16),
        ],
        compiler_params=_params("parallel", "arbitrary"),
        name="ssd_scan",
    )(xbc3, xbc3, xbc3, z3, a33, b23, pack_t, d_skip, norm_g.reshape(1, d_inner))


def _proj_ln_kernel(y_ref, h_ref, w_ref, g_ref, b_ref, o_ref):
    mix = _dot(y_ref[...], w_ref[...])
    o_ref[...] = _layer_norm(DEEPNORM_ALPHA * h_ref[...] + mix, g_ref[...], b_ref[...])


def _proj_ln(y, h, w, ln_g, ln_b):
    n_rows, d = h.shape
    k = y.shape[1]
    tm = min(TOKEN_TILE, n_rows)
    return pl.pallas_call(
        _proj_ln_kernel,
        grid=(n_rows // tm,),
        in_specs=[
            pl.BlockSpec((tm, k), lambda i: (i, 0)),
            pl.BlockSpec((tm, d), lambda i: (i, 0)),
            _const_spec(w.shape), _const_spec((1, d)), _const_spec((1, d)),
        ],
        out_specs=pl.BlockSpec((tm, d), lambda i: (i, 0)),
        out_shape=jax.ShapeDtypeStruct((n_rows, d), F32),
        compiler_params=_params("parallel"),
        name="proj_ln",
    )(y, h, w.astype(BF16), ln_g.reshape(1, d), ln_b.reshape(1, d))


def _channel_dft_table(gd):
    k = np.arange(gd)
    ang = 2.0 * np.pi * ((k[:, None] * k[None, :]) % gd) / gd
    t = np.concatenate([np.cos(ang), np.sin(ang)], axis=1) / math.sqrt(gd)
    return jnp.asarray(t, dtype=BF16)


def _seq_dft_table(s):
    tr = min(DFT_ROW_TILE, s // 2)
    n_half = s // 2 // tr
    n = jnp.arange(s, dtype=jnp.int32)
    row = jnp.arange(tr + HALO, dtype=jnp.int32)
    tile = jnp.arange(n_half, dtype=jnp.int32) * tr
    unit = 2.0 * math.pi / s
    ang_r = unit * ((row[:, None] * n[None, :]) % s).astype(F32)
    ang_t = unit * ((tile[:, None] * n[None, :]) % s).astype(F32)
    scale = 1.0 / math.sqrt(s)
    cr, sr = jnp.cos(ang_r)[None] * scale, jnp.sin(ang_r)[None] * scale
    ct, st = jnp.cos(ang_t)[:, None], jnp.sin(ang_t)[:, None]
    return jnp.concatenate([ct * cr - st * sr, st * cr + ct * sr], axis=2).astype(BF16)


def _cumsum_compaction_matrix():
    m = np.zeros((3 * V7X_LANES, 2 * V7X_LANES), np.float32)
    for g in range(SSM_GROUPS):
        for t in range(3):
            for hd in range(8):
                m[t * V7X_LANES + PACK * g + 8 + hd, 32 * g + 8 * t + hd] = 1.0
    return m


def _packed_lane_heads():
    lane = np.arange(V7X_LANES)
    g, d, r = lane // PACK, (lane // 4) % 2, lane % 4
    return d, g * 4 + r


def kernel(x, emb_ln_g, emb_ln_b, fn_w_in, fn_b_in, fn_w_out, fn_b_out, ssd_w_in, ssd_conv_w, ssd_conv_b, ssd_a_log_fwd, ssd_a_log_bwd, ssd_dt_bias_fwd, ssd_dt_bias_bwd, ssd_d, ssd_norm_g, ssd_w_out, ln_tok_g, ln_tok_b, ff_w_up, ff_conv_w, ff_conv_b, ff_w_down, ln_ffn_g, ln_ffn_b):
    bsz, s, d = x.shape
    n_rows = bsz * s
    d_inner = SSM_EXPAND * d
    n_heads = d_inner // HEAD_DIM
    conv_dim = d_inner + 2 * SSM_GROUPS * D_STATE
    assert s % TOKEN_TILE == 0 and s % CHUNK == 0 and d % (FN_GROUPS * V7X_LANES) == 0
    assert n_heads // SSM_GROUPS == 4 and SSM_GROUPS * PACK == V7X_LANES

    h = None
    for i in range(DEPTH):
        j = i // N_MIXERS
        if i % N_MIXERS == 0:
            cs = _channel_dft_table(d // FN_GROUPS)
            table = _seq_dft_table(s)
            if h is None:
                h0, z = _fnet_in(x, emb_ln_g, emb_ln_b, fn_w_in[j].astype(BF16), fn_b_in[j], cs)
            else:
                raise NotImplementedError("only the first layer uses the Fourier mixer at this depth")
            h = _fnet_seq(table, z, h0, fn_w_out[j].astype(BF16), fn_b_out[j],
                          ln_tok_g[i], ln_tok_b[i]).reshape(n_rows, d)
        else:
            w_in = ssd_w_in[j].astype(BF16)
            lane_dir, lane_head = _packed_lane_heads()
            dt_cols = d_inner + conv_dim + lane_dir * n_heads + lane_head
            w_dt = w_in[:, dt_cols]
            pick = lambda f, b: jnp.where(jnp.asarray(lane_dir == 1), b[lane_head], f[lane_head]).reshape(1, V7X_LANES)
            z, xbc, a3, b2, pack_t = _ssd_in(
                h, s, w_in, d_inner, ssd_conv_w[j], ssd_conv_b[j],
                w_dt, pick(ssd_dt_bias_fwd[j], ssd_dt_bias_bwd[j]), pick(ssd_a_log_fwd[j], ssd_a_log_bwd[j]))
            d_skip = jnp.repeat(ssd_d[j], HEAD_DIM).reshape(1, d_inner)
            yn = _ssd_scan(z, xbc, a3, b2, pack_t, d_skip, ssd_norm_g[j], bsz, s)
            h = _proj_ln(yn.reshape(n_rows, d_inner), h, ssd_w_out[j], ln_tok_g[i], ln_tok_b[i])
        h = _conv_ffn(h, s, ff_w_up[i], ff_conv_w[i], ff_conv_b[i], ff_w_down[i], ln_ffn_g[i], ln_ffn_b[i])
    return h.reshape(bsz, s, d)
```

```python
import functools
import math

import numpy as np
import jax
import jax.numpy as jnp
from jax import lax
from jax.experimental import pallas as pl
from jax.experimental.pallas import tpu as pltpu

F32 = jnp.float32
BF16 = jnp.bfloat16

LN_EPS = 1e-5
DEPTH = 2
N_MIXERS = 2
DEEPNORM_ALPHA = (2.0 * DEPTH) ** 0.25
FN_GROUPS = 8
HEAD_DIM = 64
D_STATE = 128
SSM_GROUPS = 8
SSM_EXPAND = 2
CHUNK = 128

V7X_LANES = 128
V7X_BF16_SUBLANES = 16
V7X_VMEM_LIMIT_BYTES = 56 * 1024 * 1024

TOKEN_TILE = 512
HALO = V7X_BF16_SUBLANES
FF_CHUNK = 256
XBC_CHUNK = 256
DFT_ROW_TILE = 256
PACK = 16
PROJ_AHEAD = 2
SCAN_UNROLL = (8, 8)


def _layer_norm(v, g, b):
    mu = jnp.mean(v, axis=-1, keepdims=True)
    vc = v - mu
    var = jnp.mean(vc * vc, axis=-1, keepdims=True)
    return vc * lax.rsqrt(var + LN_EPS) * g + b


def _silu(v):
    hv = 0.5 * v
    return hv + hv * jnp.tanh(hv)


def _dot(a, b):
    return jnp.dot(a, b, preferred_element_type=F32)


def _split_bf16(v, terms):
    out = []
    for _ in range(terms):
        t = v.astype(BF16)
        out.append(t)
        v = v - t.astype(F32)
    return out


def _params(*semantics):
    return pltpu.CompilerParams(dimension_semantics=semantics,
                                vmem_limit_bytes=V7X_VMEM_LIMIT_BYTES)


def _const_spec(shape):
    zeros = (0,) * len(shape)
    return pl.BlockSpec(shape, lambda *_: zeros, pipeline_mode=pl.Buffered(1))


def _fnet_in_kernel(x_ref, g_ref, b_ref, w_ref, bi_ref, cs_ref, h_ref, z_ref, *, groups):
    h = _layer_norm(x_ref[...], g_ref[...], b_ref[...])
    h_ref[...] = h
    u = (_dot(h.astype(BF16), w_ref[...]) + bi_ref[...]).astype(BF16)
    gd = u.shape[1] // groups
    for g in range(groups):
        r = _dot(u[:, g * gd:(g + 1) * gd], cs_ref[...])
        z_ref[0, :, g * gd:(g + 1) * gd] = r[:, :gd].astype(BF16)
        z_ref[1, :, g * gd:(g + 1) * gd] = r[:, gd:].astype(BF16)


def _fnet_in(x, ln_g, ln_b, w_in, b_in, cs):
    bsz, s, d = x.shape
    tm = min(TOKEN_TILE, s)
    nt = s // tm
    return pl.pallas_call(
        functools.partial(_fnet_in_kernel, groups=FN_GROUPS),
        grid=(bsz, nt),
        in_specs=[
            pl.BlockSpec((None, tm, d), lambda b, i: (b, i, 0)),
            _const_spec((1, d)), _const_spec((1, d)),
            _const_spec(w_in.shape), _const_spec((1, d)), _const_spec(cs.shape),
        ],
        out_specs=[
            pl.BlockSpec((None, tm, d), lambda b, i: (b, i, 0)),
            pl.BlockSpec((None, 2, tm, d), lambda b, i: (b, 0, i, 0)),
        ],
        out_shape=[jax.ShapeDtypeStruct((bsz, s, d), F32),
                   jax.ShapeDtypeStruct((bsz, 2, s, d), BF16)],
        compiler_params=_params("parallel", "parallel"),
        name="fnet_in",
    )(x, ln_g.reshape(1, d), ln_b.reshape(1, d), w_in, b_in.reshape(1, d), cs)


def _fnet_seq_kernel(t_ref, z_ref, h_ref, p_ref, w_ref, bo_ref, g_ref, b_ref, o_ref, y_ref, stash_ref,
                     *, n_half):
    i = pl.program_id(1)
    tr = y_ref.shape[0]
    s = z_ref.shape[0] // 2

    @pl.when(i < n_half)
    def _():
        a = _dot(t_ref[:, :s], z_ref[:s, :])
        b = _dot(t_ref[:, s:], z_ref[s:, :])
        y_ref[...] = (a - b)[:tr].astype(BF16)
        stash_ref[i] = (a + b).astype(BF16)

    @pl.when(i >= n_half)
    def _():
        y_ref[...] = _dot(p_ref[...], stash_ref[2 * n_half - 1 - i]).astype(BF16)

    mix = _dot(y_ref[...], w_ref[...]) + bo_ref[...]
    o_ref[...] = _layer_norm(DEEPNORM_ALPHA * h_ref[...] + mix, g_ref[...], b_ref[...])


def _fnet_seq(table, z, h, w_out, b_out, ln_g, ln_b):
    bsz, s, d = h.shape
    n_half, rows, _ = table.shape
    tr = rows - HALO
    z2 = z.reshape(bsz, 2 * s, d)
    flip = np.zeros((tr, rows), np.float32)
    flip[np.arange(tr), tr - np.arange(tr)] = 1.0
    return pl.pallas_call(
        functools.partial(_fnet_seq_kernel, n_half=n_half),
        grid=(bsz, 2 * n_half),
        in_specs=[
            pl.BlockSpec((None, rows, 2 * s), lambda b, i: (jnp.minimum(i, n_half - 1), 0, 0)),
            pl.BlockSpec((None, 2 * s, d), lambda b, i: (b, 0, 0), pipeline_mode=pl.Buffered(1)),
            pl.BlockSpec((None, tr, d), lambda b, i: (b, i, 0)),
            _const_spec(flip.shape), _const_spec(w_out.shape),
            _const_spec((1, d)), _const_spec((1, d)), _const_spec((1, d)),
        ],
        out_specs=pl.BlockSpec((None, tr, d), lambda b, i: (b, i, 0)),
        out_shape=jax.ShapeDtypeStruct((bsz, s, d), F32),
        scratch_shapes=[
            pltpu.VMEM((tr, d), BF16),
            pltpu.VMEM((n_half, rows, d), BF16),
        ],
        compiler_params=_params("parallel", "arbitrary"),
        name="fnet_seq",
    )(table, z2, h, jnp.asarray(flip, dtype=BF16), w_out, b_out.reshape(1, d),
      ln_g.reshape(1, d), ln_b.reshape(1, d))


def _halo_specs(tm, d, n_rows):
    per = tm // HALO
    last = n_rows // HALO - 1
    return [
        pl.BlockSpec((HALO, d), lambda i: (jnp.maximum(i * per - 1, 0), 0)),
        pl.BlockSpec((tm, d), lambda i: (i, 0)),
        pl.BlockSpec((HALO, d), lambda i: (jnp.minimum((i + 1) * per, last), 0)),
    ]


def _assemble_interleaved(xp_ref, nat_ref, hp_ref, hm_ref, hn_ref, tiles_per_seq):
    tm, d = hm_ref.shape
    r = tm + 2 * HALO
    groups = r // 8
    pos = lax.rem(pl.program_id(0), tiles_per_seq)
    keep_prev = (pos != 0).astype(F32)
    keep_next = (pos != tiles_per_seq - 1).astype(F32)
    lane_tiles = d // V7X_LANES
    for l in range(lane_tiles):
        lanes = slice(l * V7X_LANES, (l + 1) * V7X_LANES)
        nat_ref[l, 0:HALO, :] = hp_ref[:, lanes] * keep_prev
        nat_ref[l, HALO:HALO + tm, :] = hm_ref[:, lanes]
        nat_ref[l, HALO + tm:, :] = hn_ref[:, lanes] * keep_next
    gather = lambda j: jnp.concatenate(
        [nat_ref[l, pl.ds(j, 8, stride=groups), :] for l in range(lane_tiles)], axis=1)
    for j in range(0, groups, 2):
        xp_ref[8 * j:8 * j + 16, :] = jnp.concatenate([gather(j), gather(j + 1)], axis=0).astype(BF16)


def _natural_rows(val, un_ref):
    r, c = val.shape
    groups = r // 8
    lane_tiles = c // V7X_LANES
    for j in range(groups):
        for l in range(lane_tiles):
            un_ref[l, pl.ds(j, 8, stride=groups), :] = val[8 * j:8 * j + 8, l * V7X_LANES:(l + 1) * V7X_LANES]
    return jnp.concatenate([un_ref[l, HALO:r - HALO, :] for l in range(lane_tiles)], axis=1)


def _tap(u, delta):
    r = u.shape[0]
    if delta == 0:
        return u
    if delta > 0:
        wrap = pltpu.roll(u[:8 * delta], 8 * delta - 1, 0)
        return jnp.concatenate([u[8 * delta:], wrap], axis=0)
    wrap = pltpu.roll(u[r + 8 * delta:], 1, 0)
    return jnp.concatenate([wrap, u[:r + 8 * delta]], axis=0)


def _conv_taps(u, w, bias):
    width = w.shape[0]
    half = width // 2
    out = w[half:half + 1, :] * u + bias
    for k in range(width):
        if k != half:
            out = out + w[k:k + 1, :] * _tap(u, k - half)
    return out


def _conv_ffn_kernel(hp_ref, hm_ref, hn_ref, wu_ref, cw_ref, cb_ref, wd_ref, g_ref, b_ref, o_ref,
                     xp_ref, nat_ref, ug_ref, uv_ref, act_ref, *, tiles_per_seq, n_chunks):
    cw = ug_ref.shape[2]
    d_ff = n_chunks * cw
    slots = ug_ref.shape[0]
    _assemble_interleaved(xp_ref, nat_ref, hp_ref, hm_ref, hn_ref, tiles_per_seq)

    def project(c):
        gate, val = slice(c * cw, (c + 1) * cw), slice(d_ff + c * cw, d_ff + (c + 1) * cw)
        ug_ref[c % slots] = _dot(xp_ref[...], wu_ref[:, gate])
        uv_ref[c % slots] = _dot(xp_ref[...], wu_ref[:, val])

    for c in range(min(PROJ_AHEAD, n_chunks)):
        project(c)
    for c in range(n_chunks):
        if c + PROJ_AHEAD < n_chunks:
            project(c + PROJ_AHEAD)
        gate, val = slice(c * cw, (c + 1) * cw), slice(d_ff + c * cw, d_ff + (c + 1) * cw)
        cg = _conv_taps(ug_ref[c % slots], cw_ref[:, gate], cb_ref[:, gate])
        cv = _conv_taps(uv_ref[c % slots], cw_ref[:, val], cb_ref[:, val])
        act_ref[:, gate] = (_silu(cg) * cv).astype(BF16)
    ffn = _natural_rows(_dot(act_ref[...], wd_ref[...]), nat_ref)
    o_ref[...] = _layer_norm(DEEPNORM_ALPHA * hm_ref[...] + ffn, g_ref[...], b_ref[...])


def _conv_ffn(h, seq_len, w_up, conv_w, conv_b, w_down, ln_g, ln_b):
    n_rows, d = h.shape
    d_ff = w_down.shape[0]
    tm = min(TOKEN_TILE, seq_len)
    r = tm + 2 * HALO
    cw = FF_CHUNK
    nc = d_ff // cw
    assert d_ff % cw == 0 and conv_w.shape[0] // 2 <= HALO
    wu = w_up.astype(BF16)
    wd = w_down.astype(BF16)
    return pl.pallas_call(
        functools.partial(_conv_ffn_kernel, tiles_per_seq=seq_len // tm, n_chunks=nc),
        grid=(n_rows // tm,),
        in_specs=_halo_specs(tm, d, n_rows) + [
            _const_spec(wu.shape), _const_spec(conv_w.shape), _const_spec((1, 2 * d_ff)),
            _const_spec(wd.shape), _const_spec((1, d)), _const_spec((1, d)),
        ],
        out_specs=pl.BlockSpec((tm, d), lambda i: (i, 0)),
        out_shape=jax.ShapeDtypeStruct((n_rows, d), F32),
        scratch_shapes=[
            pltpu.VMEM((r, d), BF16),
            pltpu.VMEM((d // V7X_LANES, r, V7X_LANES), F32),
            pltpu.VMEM((PROJ_AHEAD + 1, r, cw), F32),
            pltpu.VMEM((PROJ_AHEAD + 1, r, cw), F32),
            pltpu.VMEM((r, d_ff), BF16),
        ],
        compiler_params=_params("parallel"),
        name="conv_ffn",
    )(h, h, h, wu, conv_w, conv_b.reshape(1, 2 * d_ff), wd, ln_g.reshape(1, d), ln_b.reshape(1, d))


def _ssd_in_kernel(hp_ref, hm_ref, hn_ref, win_ref, cw_ref, cb_ref, wdt_ref, dtb_ref, alog_ref,
                   selc_ref, z_ref, xs_ref, bm_ref, cm_ref, a3_ref, b2_ref, pt_ref,
                   xp_ref, nat_ref, xn_ref, u_ref, un_ref, *, tiles_per_seq, n_chunks):
    tm = hm_ref.shape[0]
    cw = u_ref.shape[2]
    slots = u_ref.shape[0]
    groups, _, gw = xs_ref.shape
    n_state = bm_ref.shape[2]
    d_inner = groups * gw

    def store_xbc(c, v):
        for lo in range(c * cw, (c + 1) * cw, min(gw, n_state)):
            piece = v[:, lo - c * cw:lo - c * cw + min(gw, n_state)]
            if lo < d_inner:
                xs_ref[lo // gw, :, lo % gw:lo % gw + piece.shape[1]] = piece
            elif lo < d_inner + groups * n_state:
                bm_ref[(lo - d_inner) // n_state] = piece
            else:
                cm_ref[(lo - d_inner - groups * n_state) // n_state] = piece
    _assemble_interleaved(xp_ref, nat_ref, hp_ref, hm_ref, hn_ref, tiles_per_seq)
    xn_ref[...] = hm_ref[...].astype(BF16)

    lane = lax.broadcasted_iota(jnp.int32, (1, V7X_LANES), 1)
    is_cum = (lane & 8) != 0
    is_bwd = (lane & 4) != 0
    row = lax.broadcasted_iota(jnp.int32, (CHUNK, CHUNK), 0)
    col = lax.broadcasted_iota(jnp.int32, (CHUNK, CHUNK), 1)
    tri = (row >= col).astype(BF16)
    n_sub = tm // CHUNK
    state = {}

    def dt_project():
        raw = _dot(xn_ref[...], wdt_ref[...]) + dtb_ref[...]
        state["dt"] = jnp.maximum(raw, 0.0) + jnp.log(1.0 + jnp.exp(-jnp.abs(raw)))
        state["dta"] = state["dt"] * -jnp.exp(alog_ref[...])

    def dt_cumsum(k):
        dk = state["dta"][k * CHUNK:(k + 1) * CHUNK, :]
        t0, t1, t2 = _split_bf16(dk, 3)
        state["cf", k] = _dot(tri, t2) + _dot(tri, t1) + _dot(tri, t0)

    def dt_pack(k):
        rows = slice(k * CHUNK, (k + 1) * CHUNK)
        dk, cf, dtk = state["dta"][rows, :], state["cf", k], state["dt"][rows, :]
        cb = cf[CHUNK - 1:CHUNK, :] - cf + dk
        cum = jnp.where(is_bwd, cb, cf)
        end = jnp.where(is_bwd, cum[0:1, :], cum[CHUNK - 1:CHUNK, :])
        pa = jnp.where(is_cum, cum, dtk)
        pb = jnp.where(is_cum, dtk * jnp.exp(end - cum), jnp.exp(cum))
        a3_ref[rows, :] = _dot(jnp.concatenate(_split_bf16(pa, 3), axis=1), selc_ref[...]).astype(BF16)
        for j, term in enumerate(_split_bf16(pb, 2)):
            b2_ref[rows, j * V7X_LANES:(j + 1) * V7X_LANES] = term
        pt_ref[:, rows] = pa.T

    stages = ([dt_project] + [functools.partial(dt_cumsum, k) for k in range(n_sub)]
              + [functools.partial(dt_pack, k) for k in range(n_sub)])

    def project(c):
        u_ref[c % slots] = _dot(xp_ref[...], win_ref[:, d_inner + c * cw:d_inner + (c + 1) * cw])

    z_steps = groups
    for c in range(min(PROJ_AHEAD, n_chunks)):
        project(c)
    for c in range(n_chunks):
        if c + PROJ_AHEAD < n_chunks:
            project(c + PROJ_AHEAD)
        if c < len(stages):
            stages[c]()
        for s in range(c * z_steps // n_chunks, (c + 1) * z_steps // n_chunks):
            z_ref[s] = _dot(xn_ref[...], win_ref[:, s * gw:(s + 1) * gw]).astype(BF16)
        cols = slice(c * cw, (c + 1) * cw)
        v = _silu(_conv_taps(u_ref[c % slots], cw_ref[:, cols], cb_ref[:, cols]))
        store_xbc(c, _natural_rows(v, un_ref.at[c % 2]).astype(BF16))
    for stage in stages[n_chunks:]:
        stage()


def _ssd_in(h, seq_len, w_in, d_inner, conv_w, conv_b, w_dt, dt_bias, a_log):
    n_rows, d = h.shape
    width, conv_dim = conv_w.shape
    tm = min(TOKEN_TILE, seq_len)
    r = tm + 2 * HALO
    cw = XBC_CHUNK
    nc = conv_dim // cw
    groups = SSM_GROUPS
    gw = d_inner // groups
    assert conv_dim % cw == 0 and cw % gw == 0 and width // 2 <= HALO
    assert conv_dim == d_inner + 2 * groups * D_STATE
    selc = jnp.asarray(_cumsum_compaction_matrix(), dtype=BF16)
    grouped = lambda width_: (pl.BlockSpec((groups, tm, width_), lambda i: (0, i, 0)),
                              jax.ShapeDtypeStruct((groups, n_rows, width_), BF16))
    return pl.pallas_call(
        functools.partial(_ssd_in_kernel, tiles_per_seq=seq_len // tm, n_chunks=nc),
        grid=(n_rows // tm,),
        in_specs=_halo_specs(tm, d, n_rows) + [
            _const_spec(w_in.shape), _const_spec(conv_w.shape), _const_spec((1, conv_dim)),
            _const_spec(w_dt.shape), _const_spec((1, V7X_LANES)), _const_spec((1, V7X_LANES)),
            _const_spec(selc.shape),
        ],
        out_specs=[
            grouped(gw)[0], grouped(gw)[0], grouped(D_STATE)[0], grouped(D_STATE)[0],
            pl.BlockSpec((tm, 2 * V7X_LANES), lambda i: (i, 0)),
            pl.BlockSpec((tm, 2 * V7X_LANES), lambda i: (i, 0)),
            pl.BlockSpec((V7X_LANES, tm), lambda i: (0, i)),
        ],
        out_shape=[
            grouped(gw)[1], grouped(gw)[1], grouped(D_STATE)[1], grouped(D_STATE)[1],
            jax.ShapeDtypeStruct((n_rows, 2 * V7X_LANES), BF16),
            jax.ShapeDtypeStruct((n_rows, 2 * V7X_LANES), BF16),
            jax.ShapeDtypeStruct((V7X_LANES, n_rows), F32),
        ],
        scratch_shapes=[
            pltpu.VMEM((r, d), BF16),
            pltpu.VMEM((d // V7X_LANES, r, V7X_LANES), F32),
            pltpu.VMEM((tm, d), BF16),
            pltpu.VMEM((PROJ_AHEAD + 1, r, cw), F32),
            pltpu.VMEM((2, cw // V7X_LANES, r, V7X_LANES), F32),
        ],
        compiler_params=_params("parallel"),
        name="ssd_in",
    )(h, h, h, w_in, conv_w, conv_b.reshape(1, conv_dim), w_dt, dt_bias, a_log, selc)


def _ssd_scan_kernel(x_ref, b_ref, c_ref, z_ref, a3_ref, b2_ref, pt_ref, dsk_ref, ng_ref, o_ref,
                     y_ref, sb_ref, xbd_all_ref, sel_l_ref, sel_e_ref, sel_b_ref, *, n_chunks, heads, unroll):
    q = CHUNK
    p = HEAD_DIM
    w = heads * p
    g = pl.program_id(1)
    log_q, log_p = q.bit_length() - 1, p.bit_length() - 1

    src = lax.broadcasted_iota(jnp.int32, sel_l_ref.shape, 0)
    blk = lax.broadcasted_iota(jnp.int32, sel_l_ref.shape, 1) >> log_q
    hd = 4 * (blk & 1) + (blk >> 1)
    sel_l_ref[...] = (((src >> 5) == g) & ((src & 7) == hd) & (((src >> 3) & 3) != 3)).astype(BF16)

    def selection(ref, source_offset):
        src = lax.broadcasted_iota(jnp.int32, ref.shape, 0) & (V7X_LANES - 1)
        out = lax.broadcasted_iota(jnp.int32, ref.shape, 1)
        ref[...] = (src == PACK * g + source_offset(out)).astype(BF16)

    selection(sel_e_ref, lambda j: jnp.where(j < w, j >> log_p, 8 + ((j - w) >> log_p)))
    selection(sel_b_ref, lambda j: 4 + (j >> log_p))
    @pl.when((pl.program_id(0) == 0) & (g == 0))
    def _():
        xbd_all_ref[...] = jnp.zeros_like(xbd_all_ref)

    row = lax.broadcasted_iota(jnp.int32, (q, q), 0)
    col = lax.broadcasted_iota(jnp.int32, (q, q), 1)
    lower = row >= col
    upper = row <= col
    neg_inf = jnp.float32(-jnp.inf)

    def pass1(i, hf):
        chunks = range(unroll[0])
        rows = [pl.ds(pl.multiple_of((i * unroll[0] + u) * q, q), q) for u in chunks]
        xc = [x_ref[rw, :] for rw in rows]
        bc = [b_ref[rw, :] for rw in rows]
        cc = [c_ref[rw, :] for rw in rows]
        pt = [pt_ref[:, rw] for rw in rows]
        a3 = [a3_ref[rw, :] for rw in rows]
        b2 = [b2_ref[rw, :] for rw in rows]
        cbm = [lax.dot_general(cc[u], bc[u], (((1,), (1,)), ((), ())), preferred_element_type=F32)
               for u in chunks]
        ms = [[] for _ in chunks]
        for r in range(heads):
            for u in chunks:
                cum_col = _dot(a3[u], sel_l_ref[:, 2 * r * q:2 * (r + 1) * q])
                seg_f = cum_col[:, :q] - pt[u][8 + r:9 + r, :]
                seg_b = cum_col[:, q:] - pt[u][12 + r:13 + r, :]
                lf = jnp.exp(jnp.where(lower, seg_f, neg_inf)) * pt[u][r:r + 1, :]
                lb = jnp.exp(jnp.where(upper, seg_b, neg_inf)) * pt[u][4 + r:5 + r, :]
                ms[u].append((cbm[u] * (lf + lb)).astype(BF16))
                xbd_all_ref[u, r * q:(r + 1) * q, r * p:(r + 1) * p] = xc[u][:, r * p:(r + 1) * p]
        yd = [_dot(jnp.concatenate(ms[u], axis=1), xbd_all_ref[u]) for u in chunks]
        xf = [xc[u].astype(F32) for u in chunks]
        xdec = [jnp.concatenate([(xf[u] * _dot(b2[u], sel_e_ref[:, w:2 * w])).astype(BF16),
                                 (xf[u] * _dot(b2[u], sel_e_ref[:, 2 * w:])).astype(BF16)], axis=1)
                for u in chunks]
        st = [lax.dot_general(bc[u], xdec[u], (((0,), (0,)), ((), ())), preferred_element_type=F32)
              for u in chunks]
        ef = [_dot(b2[u], sel_e_ref[:, :w]) for u in chunks]
        for u in chunks:
            y_ref[rows[u], :] = yd[u] + _dot(cc[u], hf.astype(BF16)) * ef[u] + dsk_ref[...] * xf[u]
            sb_ref[i * unroll[0] + u] = st[u][:, w:]
            hf = hf * ef[u][q - 1:q, :] + st[u][:, :w]
        return hf

    def pass2(k, hb):
        c = n_chunks - 1 - k
        rows = pl.ds(pl.multiple_of(c * q, q), q)
        eb = _dot(b2_ref[rows, :], sel_b_ref[...])
        y = y_ref[rows, :] + _dot(c_ref[rows, :], hb.astype(BF16)) * eb
        y = y * _silu(z_ref[rows, :].astype(F32))
        y = y * lax.rsqrt(jnp.mean(y * y, axis=-1, keepdims=True) + LN_EPS)
        o_ref[rows, :] = (y * ng_ref[...]).astype(BF16)
        return hb * eb[0:1, :] + sb_ref[c]

    zero = jnp.zeros((D_STATE, w), F32)
    lax.fori_loop(0, n_chunks // unroll[0], pass1, zero)
    lax.fori_loop(0, n_chunks, pass2, zero, unroll=unroll[1])


def _ssd_scan(z, xs, bm, cm, a3, b2, pack_t, d_skip, norm_g, bsz, seq_len):
    groups, n_rows, w = xs.shape
    d_inner = groups * w
    heads = w // HEAD_DIM
    n = D_STATE
    nc = seq_len // CHUNK
    assert nc % SCAN_UNROLL[0] == 0
    a33 = a3.reshape(bsz, seq_len, a3.shape[1])
    b23 = b2.reshape(bsz, seq_len, b2.shape[1])
    group_block = lambda width: pl.BlockSpec((None, seq_len, width), lambda b, g: (g, b, 0))
    return pl.pallas_call(
        functools.partial(_ssd_scan_kernel, n_chunks=nc, heads=heads, unroll=SCAN_UNROLL),
        grid=(bsz, groups),
        in_specs=[
            group_block(w), group_block(n), group_block(n), group_block(w),
            pl.BlockSpec((None, seq_len, a3.shape[1]), lambda b, g: (b, 0, 0)),
            pl.BlockSpec((None, seq_len, b2.shape[1]), lambda b, g: (b, 0, 0)),
            pl.BlockSpec((PACK, seq_len), lambda b, g: (g, b)),
            pl.BlockSpec((1, w), lambda b, g: (0, g)),
            pl.BlockSpec((1, w), lambda b, g: (0, g)),
        ],
        out_specs=group_block(w),
        out_shape=jax.ShapeDtypeStruct((groups, n_rows, w), BF16),
        scratch_shapes=[
            pltpu.VMEM((seq_len, w), F32),
            pltpu.VMEM((nc, n, w), F32),
            pltpu.VMEM((SCAN_UNROLL[0], heads * CHUNK, w), BF16),
            pltpu.VMEM((a3.shape[1], 2 * heads * CHUNK), BF16),
            pltpu.VMEM((b2.shape[1], 3 * w), BF16),
            pltpu.VMEM((b2.shape[1], w), BF16),
        ],
        compiler_params=_params("arbitrary", "arbitrary"),
        name="ssd_scan",
    )(xs, bm, cm, z, a33, b23, pack_t, d_skip, norm_g.reshape(1, d_inner))


def _proj_ln_kernel(y_ref, h_ref, w_ref, g_ref, b_ref, o_ref):
    y = jnp.concatenate([y_ref[g] for g in range(y_ref.shape[0])], axis=1)
    mix = _dot(y, w_ref[...])
    o_ref[...] = _layer_norm(DEEPNORM_ALPHA * h_ref[...] + mix, g_ref[...], b_ref[...])


def _proj_ln(y, h, w, ln_g, ln_b):
    n_rows, d = h.shape
    groups, _, gw = y.shape
    tm = min(TOKEN_TILE, n_rows)
    return pl.pallas_call(
        _proj_ln_kernel,
        grid=(n_rows // tm,),
        in_specs=[
            pl.BlockSpec((groups, tm, gw), lambda i: (0, i, 0)),
            pl.BlockSpec((tm, d), lambda i: (i, 0)),
            _const_spec(w.shape), _const_spec((1, d)), _const_spec((1, d)),
        ],
        out_specs=pl.BlockSpec((tm, d), lambda i: (i, 0)),
        out_shape=jax.ShapeDtypeStruct((n_rows, d), F32),
        compiler_params=_params("parallel"),
        name="proj_ln",
    )(y, h, w.astype(BF16), ln_g.reshape(1, d), ln_b.reshape(1, d))


def _channel_dft_table(gd):
    k = np.arange(gd)
    ang = 2.0 * np.pi * ((k[:, None] * k[None, :]) % gd) / gd
    t = np.concatenate([np.cos(ang), np.sin(ang)], axis=1) / math.sqrt(gd)
    return jnp.asarray(t, dtype=BF16)


def _seq_dft_table(s):
    tr = min(DFT_ROW_TILE, s // 2)
    n_half = s // 2 // tr
    n = jnp.arange(s, dtype=jnp.int32)
    row = jnp.arange(tr + HALO, dtype=jnp.int32)
    tile = jnp.arange(n_half, dtype=jnp.int32) * tr
    unit = 2.0 * math.pi / s
    ang_r = unit * ((row[:, None] * n[None, :]) % s).astype(F32)
    ang_t = unit * ((tile[:, None] * n[None, :]) % s).astype(F32)
    scale = 1.0 / math.sqrt(s)
    cr, sr = jnp.cos(ang_r)[None] * scale, jnp.sin(ang_r)[None] * scale
    ct, st = jnp.cos(ang_t)[:, None], jnp.sin(ang_t)[:, None]
    return jnp.concatenate([ct * cr - st * sr, st * cr + ct * sr], axis=2).astype(BF16)


def _cumsum_compaction_matrix():
    m = np.zeros((3 * V7X_LANES, 2 * V7X_LANES), np.float32)
    for g in range(SSM_GROUPS):
        for t in range(3):
            for hd in range(8):
                m[t * V7X_LANES + PACK * g + 8 + hd, 32 * g + 8 * t + hd] = 1.0
    return m


def _packed_lane_heads():
    lane = np.arange(V7X_LANES)
    g, d, r = lane // PACK, (lane // 4) % 2, lane % 4
    return d, g * 4 + r


def kernel(x, emb_ln_g, emb_ln_b, fn_w_in, fn_b_in, fn_w_out, fn_b_out, ssd_w_in, ssd_conv_w, ssd_conv_b, ssd_a_log_fwd, ssd_a_log_bwd, ssd_dt_bias_fwd, ssd_dt_bias_bwd, ssd_d, ssd_norm_g, ssd_w_out, ln_tok_g, ln_tok_b, ff_w_up, ff_conv_w, ff_conv_b, ff_w_down, ln_ffn_g, ln_ffn_b):
    bsz, s, d = x.shape
    n_rows = bsz * s
    d_inner = SSM_EXPAND * d
    n_heads = d_inner // HEAD_DIM
    conv_dim = d_inner + 2 * SSM_GROUPS * D_STATE
    assert s % TOKEN_TILE == 0 and s % CHUNK == 0 and d % (FN_GROUPS * V7X_LANES) == 0
    assert n_heads // SSM_GROUPS == 4 and SSM_GROUPS * PACK == V7X_LANES

    h = None
    for i in range(DEPTH):
        j = i // N_MIXERS
        if i % N_MIXERS == 0:
            cs = _channel_dft_table(d // FN_GROUPS)
            table = _seq_dft_table(s)
            if h is None:
                h0, z = _fnet_in(x, emb_ln_g, emb_ln_b, fn_w_in[j].astype(BF16), fn_b_in[j], cs)
            else:
                raise NotImplementedError("only the first layer uses the Fourier mixer at this depth")
            h = _fnet_seq(table, z, h0, fn_w_out[j].astype(BF16), fn_b_out[j],
                          ln_tok_g[i], ln_tok_b[i]).reshape(n_rows, d)
        else:
            w_in = ssd_w_in[j].astype(BF16)
            lane_dir, lane_head = _packed_lane_heads()
            dt_cols = d_inner + conv_dim + lane_dir * n_heads + lane_head
            w_dt = w_in[:, dt_cols]
            pick = lambda f, b: jnp.where(jnp.asarray(lane_dir == 1), b[lane_head], f[lane_head]).reshape(1, V7X_LANES)
            z, xs, bm, cm, a3, b2, pack_t = _ssd_in(
                h, s, w_in, d_inner, ssd_conv_w[j], ssd_conv_b[j],
                w_dt, pick(ssd_dt_bias_fwd[j], ssd_dt_bias_bwd[j]), pick(ssd_a_log_fwd[j], ssd_a_log_bwd[j]))
            d_skip = jnp.repeat(ssd_d[j], HEAD_DIM).reshape(1, d_inner)
            yn = _ssd_scan(z, xs, bm, cm, a3, b2, pack_t, d_skip, ssd_norm_g[j], bsz, s)
            h = _proj_ln(yn, h, ssd_w_out[j], ln_tok_g[i], ln_tok_b[i])
        h = _conv_ffn(h, s, ff_w_up[i], ff_conv_w[i], ff_conv_b[i], ff_w_down[i], ln_ffn_g[i], ln_ffn_b[i])
    return h.reshape(bsz, s, d)
```

```python
import functools
import math

import numpy as np
import jax
import jax.numpy as jnp
from jax import lax
from jax.experimental import pallas as pl
from jax.experimental.pallas import tpu as pltpu

F32 = jnp.float32
BF16 = jnp.bfloat16

LN_EPS = 1e-5
DEPTH = 2
N_MIXERS = 2
DEEPNORM_ALPHA = (2.0 * DEPTH) ** 0.25
FN_GROUPS = 8
HEAD_DIM = 64
D_STATE = 128
SSM_GROUPS = 8
SSM_EXPAND = 2
CHUNK = 128

V7X_LANES = 128
V7X_BF16_SUBLANES = 16
V7X_VMEM_LIMIT_BYTES = 56 * 1024 * 1024

TOKEN_TILE = 512
HALO = V7X_BF16_SUBLANES
FF_CHUNK = 256
XBC_CHUNK = 256
DFT_ROW_TILE = 256
PACK = 16
PROJ_AHEAD = 2
SCAN_UNROLL = (8, 8)


def _layer_norm(v, g, b):
    mu = jnp.mean(v, axis=-1, keepdims=True)
    vc = v - mu
    var = jnp.mean(vc * vc, axis=-1, keepdims=True)
    return vc * lax.rsqrt(var + LN_EPS) * g + b


def _silu(v):
    hv = 0.5 * v
    return hv + hv * jnp.tanh(hv)


def _dot(a, b):
    return jnp.dot(a, b, preferred_element_type=F32)


def _split_bf16(v, terms):
    out = []
    for _ in range(terms):
        t = v.astype(BF16)
        out.append(t)
        v = v - t.astype(F32)
    return out


def _params(*semantics):
    return pltpu.CompilerParams(dimension_semantics=semantics,
                                vmem_limit_bytes=V7X_VMEM_LIMIT_BYTES)


def _const_spec(shape):
    zeros = (0,) * len(shape)
    return pl.BlockSpec(shape, lambda *_: zeros, pipeline_mode=pl.Buffered(1))


def _layer_spec(stacked, layer):
    zeros = (0,) * (stacked.ndim - 1)
    return pl.BlockSpec((None,) + stacked.shape[1:], lambda *_: (layer,) + zeros, pipeline_mode=pl.Buffered(1))


def _fnet_in_kernel(x_ref, g_ref, b_ref, w_ref, bi_ref, cs_ref, h_ref, z_ref, *, groups):
    h = _layer_norm(x_ref[...], g_ref[...], b_ref[...])
    h_ref[...] = h
    u = (_dot(h.astype(BF16), w_ref[...]) + bi_ref[...]).astype(BF16)
    gd = u.shape[1] // groups
    for g in range(groups):
        r = _dot(u[:, g * gd:(g + 1) * gd], cs_ref[...])
        z_ref[0, :, g * gd:(g + 1) * gd] = r[:, :gd].astype(BF16)
        z_ref[1, :, g * gd:(g + 1) * gd] = r[:, gd:].astype(BF16)


def _fnet_in(x, ln_g, ln_b, w_in, b_in, cs):
    bsz, s, d = x.shape
    tm = min(TOKEN_TILE, s)
    nt = s // tm
    return pl.pallas_call(
        functools.partial(_fnet_in_kernel, groups=FN_GROUPS),
        grid=(bsz, nt),
        in_specs=[
            pl.BlockSpec((None, tm, d), lambda b, i: (b, i, 0)),
            _const_spec((1, d)), _const_spec((1, d)),
            _const_spec(w_in.shape), _const_spec((1, d)), _const_spec(cs.shape),
        ],
        out_specs=[
            pl.BlockSpec((None, tm, d), lambda b, i: (b, i, 0)),
            pl.BlockSpec((None, 2, tm, d), lambda b, i: (b, 0, i, 0)),
        ],
        out_shape=[jax.ShapeDtypeStruct((bsz, s, d), F32),
                   jax.ShapeDtypeStruct((bsz, 2, s, d), BF16)],
        compiler_params=_params("parallel", "parallel"),
        name="fnet_in",
    )(x, ln_g.reshape(1, d), ln_b.reshape(1, d), w_in, b_in.reshape(1, d), cs)


def _fnet_fold_kernel(zt_ref, zm_ref, ze_ref, p_ref, o_ref):
    keep = (pl.program_id(1) != 0).astype(F32)
    for part, sign in ((0, 1.0), (1, -1.0)):
        extra = (ze_ref[part].astype(F32) * keep).astype(BF16)
        mirrored = _dot(p_ref[...], jnp.concatenate([zm_ref[part], extra], axis=0))
        o_ref[part] = (zt_ref[part].astype(F32) + sign * mirrored).astype(BF16)


def _fnet_fold(z):
    bsz, _, s, d = z.shape
    tf = min(DFT_ROW_TILE, s // 2)
    nt = s // tf
    flip = np.zeros((tf, tf + HALO), np.float32)
    flip[np.arange(tf), tf - np.arange(tf)] = 1.0
    per = tf // HALO
    last = s // HALO - 1
    return pl.pallas_call(
        _fnet_fold_kernel,
        grid=(bsz, nt // 2),
        in_specs=[
            pl.BlockSpec((None, 2, tf, d), lambda b, j: (b, 0, j, 0)),
            pl.BlockSpec((None, 2, tf, d), lambda b, j: (b, 0, nt - 1 - j, 0)),
            pl.BlockSpec((None, 2, HALO, d), lambda b, j: (b, 0, jnp.minimum((nt - j) * per, last), 0)),
            _const_spec(flip.shape),
        ],
        out_specs=pl.BlockSpec((None, 2, tf, d), lambda b, j: (b, 0, j, 0)),
        out_shape=jax.ShapeDtypeStruct((bsz, 2, s // 2, d), BF16),
        compiler_params=_params("parallel", "parallel"),
        name="fnet_fold",
    )(z, z, z, jnp.asarray(flip, dtype=BF16))


def _fnet_seq_kernel(t_ref, z_ref, nyq_ref, h_ref, p_ref, w_ref, bo_ref, g_ref, b_ref, o_ref, y_ref, stash_ref,
                     *, n_half):
    i = pl.program_id(1)
    tr = y_ref.shape[0]
    sh = z_ref.shape[0] // 2

    @pl.when(i < n_half)
    def _():
        rows = t_ref.shape[0]
        odd = (lax.broadcasted_iota(jnp.int32, (rows, 1), 0) & 1) == 1
        nyq = nyq_ref[...].astype(F32) * (1.0 / math.sqrt(2 * sh))
        a = _dot(t_ref[:, :sh], z_ref[:sh, :]) + jnp.where(odd, -nyq, nyq)
        b = _dot(t_ref[:, sh:], z_ref[sh:, :])
        y_ref[...] = (a - b)[:tr].astype(BF16)
        stash_ref[i] = (a + b).astype(BF16)

    @pl.when(i >= n_half)
    def _():
        y_ref[...] = _dot(p_ref[...], stash_ref[2 * n_half - 1 - i]).astype(BF16)

    mix = _dot(y_ref[...], w_ref[...]) + bo_ref[...]
    o_ref[...] = _layer_norm(DEEPNORM_ALPHA * h_ref[...] + mix, g_ref[...], b_ref[...])


def _fnet_seq(table, zf, nyq, h, w_out, b_out, ln_g, ln_b):
    bsz, s, d = h.shape
    n_half, rows, _ = table.shape
    tr = rows - HALO
    assert tr % 2 == 0
    z2 = zf.reshape(bsz, s, d)
    flip = np.zeros((tr, rows), np.float32)
    flip[np.arange(tr), tr - np.arange(tr)] = 1.0
    return pl.pallas_call(
        functools.partial(_fnet_seq_kernel, n_half=n_half),
        grid=(bsz, 2 * n_half),
        in_specs=[
            pl.BlockSpec((None, rows, s), lambda b, i: (jnp.minimum(i, n_half - 1), 0, 0)),
            pl.BlockSpec((None, s, d), lambda b, i: (b, 0, 0)),
            pl.BlockSpec((None, 1, d), lambda b, i: (b, 0, 0)),
            pl.BlockSpec((None, tr, d), lambda b, i: (b, i, 0)),
            _const_spec(flip.shape), _const_spec(w_out.shape),
            _const_spec((1, d)), _const_spec((1, d)), _const_spec((1, d)),
        ],
        out_specs=pl.BlockSpec((None, tr, d), lambda b, i: (b, i, 0)),
        out_shape=jax.ShapeDtypeStruct((bsz, s, d), F32),
        scratch_shapes=[
            pltpu.VMEM((tr, d), BF16),
            pltpu.VMEM((n_half, rows, d), BF16),
        ],
        compiler_params=_params("parallel", "arbitrary"),
        name="fnet_seq",
    )(table, z2, nyq, h, jnp.asarray(flip, dtype=BF16), w_out, b_out.reshape(1, d),
      ln_g.reshape(1, d), ln_b.reshape(1, d))


def _halo_specs(tm, d, n_rows):
    per = tm // HALO
    last = n_rows // HALO - 1
    return [
        pl.BlockSpec((HALO, d), lambda i: (jnp.maximum(i * per - 1, 0), 0)),
        pl.BlockSpec((tm, d), lambda i: (i, 0)),
        pl.BlockSpec((HALO, d), lambda i: (jnp.minimum((i + 1) * per, last), 0)),
    ]


def _assemble_interleaved(xp_ref, nat_ref, hp_ref, hm_ref, hn_ref, tiles_per_seq):
    tm, d = hm_ref.shape
    r = tm + 2 * HALO
    groups = r // 8
    pos = lax.rem(pl.program_id(0), tiles_per_seq)
    keep_prev = (pos != 0).astype(F32)
    keep_next = (pos != tiles_per_seq - 1).astype(F32)
    lane_tiles = d // V7X_LANES
    for l in range(lane_tiles):
        lanes = slice(l * V7X_LANES, (l + 1) * V7X_LANES)
        nat_ref[l, 0:HALO, :] = hp_ref[:, lanes] * keep_prev
        nat_ref[l, HALO:HALO + tm, :] = hm_ref[:, lanes]
        nat_ref[l, HALO + tm:, :] = hn_ref[:, lanes] * keep_next
    gather = lambda j: jnp.concatenate(
        [nat_ref[l, pl.ds(j, 8, stride=groups), :] for l in range(lane_tiles)], axis=1)
    for j in range(0, groups, 2):
        xp_ref[8 * j:8 * j + 16, :] = jnp.concatenate([gather(j), gather(j + 1)], axis=0).astype(BF16)


def _natural_rows(val, un_ref):
    r, c = val.shape
    groups = r // 8
    lane_tiles = c // V7X_LANES
    for j in range(groups):
        for l in range(lane_tiles):
            un_ref[l, pl.ds(j, 8, stride=groups), :] = val[8 * j:8 * j + 8, l * V7X_LANES:(l + 1) * V7X_LANES]
    return jnp.concatenate([un_ref[l, HALO:r - HALO, :] for l in range(lane_tiles)], axis=1)


def _tap(u, delta):
    r = u.shape[0]
    if delta == 0:
        return u
    if delta > 0:
        wrap = pltpu.roll(u[:8 * delta], 8 * delta - 1, 0)
        return jnp.concatenate([u[8 * delta:], wrap], axis=0)
    wrap = pltpu.roll(u[r + 8 * delta:], 1, 0)
    return jnp.concatenate([wrap, u[:r + 8 * delta]], axis=0)


def _conv_taps(u, w, bias):
    width = w.shape[0]
    half = width // 2
    out = w[half:half + 1, :] * u + bias
    for k in range(width):
        if k != half:
            out = out + w[k:k + 1, :] * _tap(u, k - half)
    return out


def _conv_ffn_kernel(hp_ref, hm_ref, hn_ref, wu_ref, cw_ref, cb_ref, wd_ref, g_ref, b_ref, o_ref,
                     xp_ref, nat_ref, ug_ref, uv_ref, act_ref, *, tiles_per_seq, n_chunks):
    cw = ug_ref.shape[2]
    d_ff = n_chunks * cw
    slots = ug_ref.shape[0]
    _assemble_interleaved(xp_ref, nat_ref, hp_ref, hm_ref, hn_ref, tiles_per_seq)

    def project(c):
        gate, val = slice(c * cw, (c + 1) * cw), slice(d_ff + c * cw, d_ff + (c + 1) * cw)
        ug_ref[c % slots] = _dot(xp_ref[...], wu_ref[:, gate])
        uv_ref[c % slots] = _dot(xp_ref[...], wu_ref[:, val])

    for c in range(min(PROJ_AHEAD, n_chunks)):
        project(c)
    for c in range(n_chunks):
        if c + PROJ_AHEAD < n_chunks:
            project(c + PROJ_AHEAD)
        gate, val = slice(c * cw, (c + 1) * cw), slice(d_ff + c * cw, d_ff + (c + 1) * cw)
        cg = _conv_taps(ug_ref[c % slots], cw_ref[:, gate], cb_ref[:, gate])
        cv = _conv_taps(uv_ref[c % slots], cw_ref[:, val], cb_ref[:, val])
        act_ref[:, gate] = (_silu(cg) * cv).astype(BF16)
    ffn = _natural_rows(_dot(act_ref[...], wd_ref[...]), nat_ref)
    o_ref[...] = _layer_norm(DEEPNORM_ALPHA * hm_ref[...] + ffn, g_ref[...], b_ref[...])


def _conv_ffn(h, seq_len, layer, w_up, conv_w, conv_b, w_down, ln_g, ln_b):
    n_rows, d = h.shape
    d_ff = w_down.shape[1]
    tm = min(TOKEN_TILE, seq_len)
    r = tm + 2 * HALO
    cw = FF_CHUNK
    nc = d_ff // cw
    assert d_ff % cw == 0 and conv_w.shape[1] // 2 <= HALO
    conv_b3 = conv_b.reshape(conv_b.shape[0], 1, 2 * d_ff)
    return pl.pallas_call(
        functools.partial(_conv_ffn_kernel, tiles_per_seq=seq_len // tm, n_chunks=nc),
        grid=(n_rows // tm,),
        in_specs=_halo_specs(tm, d, n_rows) + [
            _layer_spec(w_up, layer), _layer_spec(conv_w, layer), _layer_spec(conv_b3, layer),
            _layer_spec(w_down, layer), _const_spec((1, d)), _const_spec((1, d)),
        ],
        out_specs=pl.BlockSpec((tm, d), lambda i: (i, 0)),
        out_shape=jax.ShapeDtypeStruct((n_rows, d), F32),
        scratch_shapes=[
            pltpu.VMEM((r, d), BF16),
            pltpu.VMEM((d // V7X_LANES, r, V7X_LANES), F32),
            pltpu.VMEM((PROJ_AHEAD + 1, r, cw), F32),
            pltpu.VMEM((PROJ_AHEAD + 1, r, cw), F32),
            pltpu.VMEM((r, d_ff), BF16),
        ],
        compiler_params=_params("parallel"),
        name="conv_ffn",
    )(h, h, h, w_up, conv_w, conv_b3, w_down, ln_g.reshape(1, d), ln_b.reshape(1, d))


def _ssd_in_kernel(hp_ref, hm_ref, hn_ref, win_ref, cw_ref, cb_ref, wdt_ref, dtb_ref, alog_ref,
                   selc_ref, z_ref, xbc_ref, a3_ref, b2_ref, pt_ref, xp_ref, nat_ref, xn_ref, u_ref, un_ref,
                   *, tiles_per_seq, n_chunks):
    tm = hm_ref.shape[0]
    cw = u_ref.shape[2]
    slots = u_ref.shape[0]
    d_inner = z_ref.shape[1]
    _assemble_interleaved(xp_ref, nat_ref, hp_ref, hm_ref, hn_ref, tiles_per_seq)
    xn_ref[...] = hm_ref[...].astype(BF16)

    lane = lax.broadcasted_iota(jnp.int32, (1, V7X_LANES), 1)
    is_cum = (lane & 8) != 0
    is_bwd = (lane & 4) != 0
    row = lax.broadcasted_iota(jnp.int32, (CHUNK, CHUNK), 0)
    col = lax.broadcasted_iota(jnp.int32, (CHUNK, CHUNK), 1)
    tri = (row >= col).astype(BF16)
    n_sub = tm // CHUNK
    state = {}

    def dt_project():
        raw = _dot(xn_ref[...], wdt_ref[...]) + dtb_ref[...]
        state["dt"] = jnp.maximum(raw, 0.0) + jnp.log(1.0 + jnp.exp(-jnp.abs(raw)))
        state["dta"] = state["dt"] * -jnp.exp(alog_ref[...])

    def dt_cumsum(k):
        dk = state["dta"][k * CHUNK:(k + 1) * CHUNK, :]
        t0, t1, t2 = _split_bf16(dk, 3)
        state["cf", k] = _dot(tri, t2) + _dot(tri, t1) + _dot(tri, t0)

    def dt_pack(k):
        rows = slice(k * CHUNK, (k + 1) * CHUNK)
        dk, cf, dtk = state["dta"][rows, :], state["cf", k], state["dt"][rows, :]
        cb = cf[CHUNK - 1:CHUNK, :] - cf + dk
        cum = jnp.where(is_bwd, cb, cf)
        end = jnp.where(is_bwd, cum[0:1, :], cum[CHUNK - 1:CHUNK, :])
        pa = jnp.where(is_cum, cum, dtk)
        pb = jnp.where(is_cum, dtk * jnp.exp(end - cum), jnp.exp(cum))
        a3_ref[rows, :] = _dot(jnp.concatenate(_split_bf16(pa, 3), axis=1), selc_ref[...]).astype(BF16)
        for j, term in enumerate(_split_bf16(pb, 2)):
            b2_ref[rows, j * V7X_LANES:(j + 1) * V7X_LANES] = term
        pt_ref[:, rows] = pa.T

    stages = ([dt_project] + [functools.partial(dt_cumsum, k) for k in range(n_sub)]
              + [functools.partial(dt_pack, k) for k in range(n_sub)])

    def project(c):
        u_ref[c % slots] = _dot(xp_ref[...], win_ref[:, d_inner + c * cw:d_inner + (c + 1) * cw])

    z_steps = d_inner // cw
    for c in range(min(PROJ_AHEAD, n_chunks)):
        project(c)
    for c in range(n_chunks):
        if c + PROJ_AHEAD < n_chunks:
            project(c + PROJ_AHEAD)
        if c < len(stages):
            stages[c]()
        for s in range(c * z_steps // n_chunks, (c + 1) * z_steps // n_chunks):
            zc = slice(s * cw, (s + 1) * cw)
            z_ref[:, zc] = _dot(xn_ref[...], win_ref[:, zc]).astype(BF16)
        cols = slice(c * cw, (c + 1) * cw)
        v = _silu(_conv_taps(u_ref[c % slots], cw_ref[:, cols], cb_ref[:, cols]))
        xbc_ref[:, cols] = _natural_rows(v, un_ref.at[c % 2]).astype(BF16)
    for stage in stages[n_chunks:]:
        stage()


def _ssd_in(h, seq_len, w_in, d_inner, conv_w, conv_b, w_dt, dt_bias, a_log):
    n_rows, d = h.shape
    width, conv_dim = conv_w.shape
    tm = min(TOKEN_TILE, seq_len)
    r = tm + 2 * HALO
    cw = XBC_CHUNK
    nc = conv_dim // cw
    assert conv_dim % cw == 0 and d_inner % V7X_LANES == 0 and width // 2 <= HALO
    selc = jnp.asarray(_cumsum_compaction_matrix(), dtype=BF16)
    return pl.pallas_call(
        functools.partial(_ssd_in_kernel, tiles_per_seq=seq_len // tm, n_chunks=nc),
        grid=(n_rows // tm,),
        in_specs=_halo_specs(tm, d, n_rows) + [
            _const_spec(w_in.shape), _const_spec(conv_w.shape), _const_spec((1, conv_dim)),
            _const_spec(w_dt.shape), _const_spec((1, V7X_LANES)), _const_spec((1, V7X_LANES)),
            _const_spec(selc.shape),
        ],
        out_specs=[
            pl.BlockSpec((tm, d_inner), lambda i: (i, 0)),
            pl.BlockSpec((tm, conv_dim), lambda i: (i, 0)),
            pl.BlockSpec((tm, 2 * V7X_LANES), lambda i: (i, 0)),
            pl.BlockSpec((tm, 2 * V7X_LANES), lambda i: (i, 0)),
            pl.BlockSpec((V7X_LANES, tm), lambda i: (0, i)),
        ],
        out_shape=[
            jax.ShapeDtypeStruct((n_rows, d_inner), BF16),
            jax.ShapeDtypeStruct((n_rows, conv_dim), BF16),
            jax.ShapeDtypeStruct((n_rows, 2 * V7X_LANES), BF16),
            jax.ShapeDtypeStruct((n_rows, 2 * V7X_LANES), BF16),
            jax.ShapeDtypeStruct((V7X_LANES, n_rows), F32),
        ],
        scratch_shapes=[
            pltpu.VMEM((r, d), BF16),
            pltpu.VMEM((d // V7X_LANES, r, V7X_LANES), F32),
            pltpu.VMEM((tm, d), BF16),
            pltpu.VMEM((PROJ_AHEAD + 1, r, cw), F32),
            pltpu.VMEM((2, cw // V7X_LANES, r, V7X_LANES), F32),
        ],
        compiler_params=_params("parallel"),
        name="ssd_in",
    )(h, h, h, w_in, conv_w, conv_b.reshape(1, conv_dim), w_dt, dt_bias, a_log, selc)


def _ssd_scan_kernel(x_ref, b_ref, c_ref, z_ref, a3_ref, b2_ref, pt_ref, dsk_ref, ng_ref, o_ref,
                     y_ref, sb_ref, xbd_all_ref, sel_l_ref, sel_e_ref, sel_b_ref, *, n_chunks, heads, unroll):
    q = CHUNK
    p = HEAD_DIM
    w = heads * p
    g = pl.program_id(1)
    log_q, log_p = q.bit_length() - 1, p.bit_length() - 1

    src = lax.broadcasted_iota(jnp.int32, sel_l_ref.shape, 0)
    blk = lax.broadcasted_iota(jnp.int32, sel_l_ref.shape, 1) >> log_q
    hd = 4 * (blk & 1) + (blk >> 1)
    sel_l_ref[...] = (((src >> 5) == g) & ((src & 7) == hd) & (((src >> 3) & 3) != 3)).astype(BF16)

    def selection(ref, source_offset):
        src = lax.broadcasted_iota(jnp.int32, ref.shape, 0) & (V7X_LANES - 1)
        out = lax.broadcasted_iota(jnp.int32, ref.shape, 1)
        ref[...] = (src == PACK * g + source_offset(out)).astype(BF16)

    selection(sel_e_ref, lambda j: jnp.where(j < w, j >> log_p, 8 + ((j - w) >> log_p)))
    selection(sel_b_ref, lambda j: 4 + (j >> log_p))

    @pl.when((pl.program_id(0) == 0) & (g == 0))
    def _():
        xbd_all_ref[...] = jnp.zeros_like(xbd_all_ref)

    row = lax.broadcasted_iota(jnp.int32, (q, q), 0)
    col = lax.broadcasted_iota(jnp.int32, (q, q), 1)
    lower = row >= col
    upper = row <= col
    neg_inf = jnp.float32(-jnp.inf)

    def pass1(i, hf):
        chunks = range(unroll[0])
        rows = [pl.ds(pl.multiple_of((i * unroll[0] + u) * q, q), q) for u in chunks]
        xc = [x_ref[rw, :] for rw in rows]
        bc = [b_ref[rw, :] for rw in rows]
        cc = [c_ref[rw, :] for rw in rows]
        pt = [pt_ref[:, rw] for rw in rows]
        a3 = [a3_ref[rw, :] for rw in rows]
        b2 = [b2_ref[rw, :] for rw in rows]
        cbm = [lax.dot_general(cc[u], bc[u], (((1,), (1,)), ((), ())), preferred_element_type=F32)
               for u in chunks]
        ms = [[] for _ in chunks]
        for r in range(heads):
            for u in chunks:
                cum_col = _dot(a3[u], sel_l_ref[:, 2 * r * q:2 * (r + 1) * q])
                seg_f = cum_col[:, :q] - pt[u][8 + r:9 + r, :]
                seg_b = cum_col[:, q:] - pt[u][12 + r:13 + r, :]
                lf = jnp.exp(jnp.where(lower, seg_f, neg_inf)) * pt[u][r:r + 1, :]
                lb = jnp.exp(jnp.where(upper, seg_b, neg_inf)) * pt[u][4 + r:5 + r, :]
                ms[u].append((cbm[u] * (lf + lb)).astype(BF16))
                xbd_all_ref[u, r * q:(r + 1) * q, r * p:(r + 1) * p] = xc[u][:, r * p:(r + 1) * p]
        yd = [_dot(jnp.concatenate(ms[u], axis=1), xbd_all_ref[u]) for u in chunks]
        xf = [xc[u].astype(F32) for u in chunks]
        xdec = [jnp.concatenate([(xf[u] * _dot(b2[u], sel_e_ref[:, w:2 * w])).astype(BF16),
                                 (xf[u] * _dot(b2[u], sel_e_ref[:, 2 * w:])).astype(BF16)], axis=1)
                for u in chunks]
        st = [lax.dot_general(bc[u], xdec[u], (((0,), (0,)), ((), ())), preferred_element_type=F32)
              for u in chunks]
        ef = [_dot(b2[u], sel_e_ref[:, :w]) for u in chunks]
        for u in chunks:
            y_ref[rows[u], :] = yd[u] + _dot(cc[u], hf.astype(BF16)) * ef[u] + dsk_ref[...] * xf[u]
            sb_ref[i * unroll[0] + u] = st[u][:, w:]
            hf = hf * ef[u][q - 1:q, :] + st[u][:, :w]
        return hf

    def pass2(k, hb):
        c = n_chunks - 1 - k
        rows = pl.ds(pl.multiple_of(c * q, q), q)
        eb = _dot(b2_ref[rows, :], sel_b_ref[...])
        y = y_ref[rows, :] + _dot(c_ref[rows, :], hb.astype(BF16)) * eb
        y = y * _silu(z_ref[rows, :].astype(F32))
        y = y * lax.rsqrt(jnp.mean(y * y, axis=-1, keepdims=True) + LN_EPS)
        o_ref[rows, :] = (y * ng_ref[...]).astype(BF16)
        return hb * eb[0:1, :] + sb_ref[c]

    zero = jnp.zeros((D_STATE, w), F32)
    lax.fori_loop(0, n_chunks // unroll[0], pass1, zero)
    lax.fori_loop(0, n_chunks, pass2, zero, unroll=unroll[1])


def _ssd_scan(z, xbc, a3, b2, pack_t, d_skip, norm_g, bsz, seq_len):
    d_inner = z.shape[1]
    groups = SSM_GROUPS
    w = d_inner // groups
    heads = w // HEAD_DIM
    n = D_STATE
    nc = seq_len // CHUNK
    assert nc % SCAN_UNROLL[0] == 0
    z3 = z.reshape(bsz, seq_len, d_inner)
    xbc3 = xbc.reshape(bsz, seq_len, xbc.shape[1])
    a33 = a3.reshape(bsz, seq_len, a3.shape[1])
    b23 = b2.reshape(bsz, seq_len, b2.shape[1])
    b_off = d_inner // n
    c_off = b_off + groups
    return pl.pallas_call(
        functools.partial(_ssd_scan_kernel, n_chunks=nc, heads=heads, unroll=SCAN_UNROLL),
        grid=(bsz, groups),
        in_specs=[
            pl.BlockSpec((None, seq_len, w), lambda b, g: (b, 0, g)),
            pl.BlockSpec((None, seq_len, n), lambda b, g: (b, 0, b_off + g)),
            pl.BlockSpec((None, seq_len, n), lambda b, g: (b, 0, c_off + g)),
            pl.BlockSpec((None, seq_len, w), lambda b, g: (b, 0, g)),
            pl.BlockSpec((None, seq_len, a3.shape[1]), lambda b, g: (b, 0, 0)),
            pl.BlockSpec((None, seq_len, b2.shape[1]), lambda b, g: (b, 0, 0)),
            pl.BlockSpec((PACK, seq_len), lambda b, g: (g, b)),
            pl.BlockSpec((1, w), lambda b, g: (0, g)),
            pl.BlockSpec((1, w), lambda b, g: (0, g)),
        ],
        out_specs=pl.BlockSpec((None, seq_len, w), lambda b, g: (b, 0, g)),
        out_shape=jax.ShapeDtypeStruct((bsz, seq_len, d_inner), BF16),
        scratch_shapes=[
            pltpu.VMEM((seq_len, w), F32),
            pltpu.VMEM((nc, n, w), F32),
            pltpu.VMEM((SCAN_UNROLL[0], heads * CHUNK, w), BF16),
            pltpu.VMEM((a3.shape[1], 2 * heads * CHUNK), BF16),
            pltpu.VMEM((b2.shape[1], 3 * w), BF16),
            pltpu.VMEM((b2.shape[1], w), BF16),
        ],
        compiler_params=_params("arbitrary", "arbitrary"),
        name="ssd_scan",
    )(xbc3, xbc3, xbc3, z3, a33, b23, pack_t, d_skip, norm_g.reshape(1, d_inner))


def _proj_ln_kernel(y_ref, h_ref, w_ref, g_ref, b_ref, o_ref):
    mix = _dot(y_ref[...], w_ref[...])
    o_ref[...] = _layer_norm(DEEPNORM_ALPHA * h_ref[...] + mix, g_ref[...], b_ref[...])


def _proj_ln(y, h, w, ln_g, ln_b):
    n_rows, d = h.shape
    k = y.shape[1]
    tm = min(TOKEN_TILE, n_rows)
    return pl.pallas_call(
        _proj_ln_kernel,
        grid=(n_rows // tm,),
        in_specs=[
            pl.BlockSpec((tm, k), lambda i: (i, 0)),
            pl.BlockSpec((tm, d), lambda i: (i, 0)),
            _const_spec(w.shape), _const_spec((1, d)), _const_spec((1, d)),
        ],
        out_specs=pl.BlockSpec((tm, d), lambda i: (i, 0)),
        out_shape=jax.ShapeDtypeStruct((n_rows, d), F32),
        compiler_params=_params("parallel"),
        name="proj_ln",
    )(y, h, w.astype(BF16), ln_g.reshape(1, d), ln_b.reshape(1, d))


def _channel_dft_table(gd):
    k = np.arange(gd)
    ang = 2.0 * np.pi * ((k[:, None] * k[None, :]) % gd) / gd
    t = np.concatenate([np.cos(ang), np.sin(ang)], axis=1) / math.sqrt(gd)
    return jnp.asarray(t, dtype=BF16)


def _seq_dft_table(s):
    tr = min(DFT_ROW_TILE, s // 2)
    n_half = s // 2 // tr
    n = jnp.arange(s // 2, dtype=jnp.int32)
    row = jnp.arange(tr + HALO, dtype=jnp.int32)
    tile = jnp.arange(n_half, dtype=jnp.int32) * tr
    unit = 2.0 * math.pi / s
    ang_r = unit * ((row[:, None] * n[None, :]) % s).astype(F32)
    ang_t = unit * ((tile[:, None] * n[None, :]) % s).astype(F32)
    scale = 1.0 / math.sqrt(s)
    cr, sr = jnp.cos(ang_r)[None] * scale, jnp.sin(ang_r)[None] * scale
    ct, st = jnp.cos(ang_t)[:, None], jnp.sin(ang_t)[:, None]
    return jnp.concatenate([ct * cr - st * sr, st * cr + ct * sr], axis=2).astype(BF16)


def _cumsum_compaction_matrix():
    m = np.zeros((3 * V7X_LANES, 2 * V7X_LANES), np.float32)
    for g in range(SSM_GROUPS):
        for t in range(3):
            for hd in range(8):
                m[t * V7X_LANES + PACK * g + 8 + hd, 32 * g + 8 * t + hd] = 1.0
    return m


def _packed_lane_heads():
    lane = np.arange(V7X_LANES)
    g, d, r = lane // PACK, (lane // 4) % 2, lane % 4
    return d, g * 4 + r


def kernel(x, emb_ln_g, emb_ln_b, fn_w_in, fn_b_in, fn_w_out, fn_b_out, ssd_w_in, ssd_conv_w, ssd_conv_b, ssd_a_log_fwd, ssd_a_log_bwd, ssd_dt_bias_fwd, ssd_dt_bias_bwd, ssd_d, ssd_norm_g, ssd_w_out, ln_tok_g, ln_tok_b, ff_w_up, ff_conv_w, ff_conv_b, ff_w_down, ln_ffn_g, ln_ffn_b):
    bsz, s, d = x.shape
    n_rows = bsz * s
    d_inner = SSM_EXPAND * d
    n_heads = d_inner // HEAD_DIM
    conv_dim = d_inner + 2 * SSM_GROUPS * D_STATE
    assert s % TOKEN_TILE == 0 and s % CHUNK == 0 and d % (FN_GROUPS * V7X_LANES) == 0
    assert n_heads // SSM_GROUPS == 4 and SSM_GROUPS * PACK == V7X_LANES

    ff_w_up_b, ff_w_down_b = ff_w_up.astype(BF16), ff_w_down.astype(BF16)
    h = None
    for i in range(DEPTH):
        j = i // N_MIXERS
        if i % N_MIXERS == 0:
            cs = _channel_dft_table(d // FN_GROUPS)
            table = _seq_dft_table(s)
            if h is None:
                h0, z = _fnet_in(x, emb_ln_g, emb_ln_b, fn_w_in[j].astype(BF16), fn_b_in[j], cs)
            else:
                raise NotImplementedError("only the first layer uses the Fourier mixer at this depth")
            nyq = z[:, 0, s // 2:s // 2 + 1, :]
            h = _fnet_seq(table, _fnet_fold(z), nyq, h0, fn_w_out[j].astype(BF16), fn_b_out[j],
                          ln_tok_g[i], ln_tok_b[i]).reshape(n_rows, d)
        else:
            w_in = ssd_w_in[j].astype(BF16)
            lane_dir, lane_head = _packed_lane_heads()
            w_dt = ssd_w_in[j][:, d_inner + conv_dim:][:, lane_dir * n_heads + lane_head].astype(BF16)
            pick = lambda f, b: jnp.where(jnp.asarray(lane_dir == 1), b[lane_head], f[lane_head]).reshape(1, V7X_LANES)
            z, xbc, a3, b2, pack_t = _ssd_in(
                h, s, w_in, d_inner, ssd_conv_w[j], ssd_conv_b[j],
                w_dt, pick(ssd_dt_bias_fwd[j], ssd_dt_bias_bwd[j]), pick(ssd_a_log_fwd[j], ssd_a_log_bwd[j]))
            d_skip = jnp.repeat(ssd_d[j], HEAD_DIM).reshape(1, d_inner)
            yn = _ssd_scan(z, xbc, a3, b2, pack_t, d_skip, ssd_norm_g[j], bsz, s)
            h = _proj_ln(yn.reshape(n_rows, d_inner), h, ssd_w_out[j], ln_tok_g[i], ln_tok_b[i])
        h = _conv_ffn(h, s, i, ff_w_up_b, ff_conv_w, ff_conv_b, ff_w_down_b, ln_ffn_g[i], ln_ffn_b[i])
    return h.reshape(bsz, s, d)
```

```python
import functools
import math

import numpy as np
import jax
import jax.numpy as jnp
from jax import lax
from jax.experimental import pallas as pl
from jax.experimental.pallas import tpu as pltpu

F32 = jnp.float32
BF16 = jnp.bfloat16

LN_EPS = 1e-5
DEPTH = 2
N_MIXERS = 2
DEEPNORM_ALPHA = (2.0 * DEPTH) ** 0.25
FN_GROUPS = 8
HEAD_DIM = 64
D_STATE = 128
SSM_GROUPS = 8
SSM_EXPAND = 2
CHUNK = 128

V7X_LANES = 128
V7X_BF16_SUBLANES = 16
V7X_VMEM_LIMIT_BYTES = 56 * 1024 * 1024

TOKEN_TILE = 512
PLAIN_TILE = 1024
HALO = V7X_BF16_SUBLANES
FF_CHUNK = 256
XBC_CHUNK = 256
DFT_ROW_TILE = 512
PACK = 16
PROJ_AHEAD = 2
SCAN_UNROLL = (8, 8)


def _layer_norm(v, g, b):
    mu = jnp.mean(v, axis=-1, keepdims=True)
    vc = v - mu
    var = jnp.mean(vc * vc, axis=-1, keepdims=True)
    return vc * lax.rsqrt(var + LN_EPS) * g + b


def _silu(v):
    hv = 0.5 * v
    return hv + hv * jnp.tanh(hv)


def _dot(a, b):
    return jnp.dot(a, b, preferred_element_type=F32)


def _split_bf16(v, terms):
    out = []
    for _ in range(terms):
        t = v.astype(BF16)
        out.append(t)
        v = v - t.astype(F32)
    return out


def _params(*semantics):
    return pltpu.CompilerParams(dimension_semantics=semantics,
                                vmem_limit_bytes=V7X_VMEM_LIMIT_BYTES)


def _const_spec(shape):
    zeros = (0,) * len(shape)
    return pl.BlockSpec(shape, lambda *_: zeros, pipeline_mode=pl.Buffered(1))


def _layer_spec(stacked, layer):
    zeros = (0,) * (stacked.ndim - 1)
    return pl.BlockSpec((None,) + stacked.shape[1:], lambda *_: (layer,) + zeros, pipeline_mode=pl.Buffered(1))


def _fnet_in_kernel(x_ref, g_ref, b_ref, w_ref, bi_ref, cs_ref, h_ref, z_ref, *, groups):
    h = _layer_norm(x_ref[...], g_ref[...], b_ref[...])
    h_ref[...] = h
    u = (_dot(h.astype(BF16), w_ref[...]) + bi_ref[...]).astype(BF16)
    gd = u.shape[1] // groups
    for g in range(groups):
        r = _dot(u[:, g * gd:(g + 1) * gd], cs_ref[...])
        z_ref[0, :, g * gd:(g + 1) * gd] = r[:, :gd].astype(BF16)
        z_ref[1, :, g * gd:(g + 1) * gd] = r[:, gd:].astype(BF16)


def _fnet_in(x, ln_g, ln_b, w_in, b_in, cs):
    bsz, s, d = x.shape
    tm = min(PLAIN_TILE, s)
    nt = s // tm
    return pl.pallas_call(
        functools.partial(_fnet_in_kernel, groups=FN_GROUPS),
        grid=(bsz, nt),
        in_specs=[
            pl.BlockSpec((None, tm, d), lambda b, i: (b, i, 0)),
            _const_spec((1, d)), _const_spec((1, d)),
            _const_spec(w_in.shape), _const_spec((1, d)), _const_spec(cs.shape),
        ],
        out_specs=[
            pl.BlockSpec((None, tm, d), lambda b, i: (b, i, 0)),
            pl.BlockSpec((None, 2, tm, d), lambda b, i: (b, 0, i, 0)),
        ],
        out_shape=[jax.ShapeDtypeStruct((bsz, s, d), F32),
                   jax.ShapeDtypeStruct((bsz, 2, s, d), BF16)],
        compiler_params=_params("parallel", "parallel"),
        name="fnet_in",
    )(x, ln_g.reshape(1, d), ln_b.reshape(1, d), w_in, b_in.reshape(1, d), cs)


def _fnet_fold_kernel(zt_ref, zm_ref, ze_ref, p_ref, o_ref):
    keep = (pl.program_id(1) != 0).astype(F32)
    for part, sign in ((0, 1.0), (1, -1.0)):
        extra = (ze_ref[part].astype(F32) * keep).astype(BF16)
        mirrored = _dot(p_ref[...], jnp.concatenate([zm_ref[part], extra], axis=0))
        o_ref[part] = (zt_ref[part].astype(F32) + sign * mirrored).astype(BF16)


def _fnet_fold(z):
    bsz, _, s, d = z.shape
    tf = min(DFT_ROW_TILE, s // 2)
    nt = s // tf
    flip = np.zeros((tf, tf + HALO), np.float32)
    flip[np.arange(tf), tf - np.arange(tf)] = 1.0
    per = tf // HALO
    last = s // HALO - 1
    return pl.pallas_call(
        _fnet_fold_kernel,
        grid=(bsz, nt // 2),
        in_specs=[
            pl.BlockSpec((None, 2, tf, d), lambda b, j: (b, 0, j, 0)),
            pl.BlockSpec((None, 2, tf, d), lambda b, j: (b, 0, nt - 1 - j, 0)),
            pl.BlockSpec((None, 2, HALO, d), lambda b, j: (b, 0, jnp.minimum((nt - j) * per, last), 0)),
            _const_spec(flip.shape),
        ],
        out_specs=pl.BlockSpec((None, 2, tf, d), lambda b, j: (b, 0, j, 0)),
        out_shape=jax.ShapeDtypeStruct((bsz, 2, s // 2, d), BF16),
        compiler_params=_params("parallel", "parallel"),
        name="fnet_fold",
    )(z, z, z, jnp.asarray(flip, dtype=BF16))


def _fnet_seq_kernel(t_ref, z_ref, nyq_ref, h_ref, p_ref, w_ref, bo_ref, g_ref, b_ref, o_ref, y_ref, stash_ref,
                     *, n_half):
    i = pl.program_id(1)
    tr = y_ref.shape[0]
    sh = z_ref.shape[0] // 2

    @pl.when(i < n_half)
    def _():
        rows = t_ref.shape[0]
        odd = (lax.broadcasted_iota(jnp.int32, (rows, 1), 0) & 1) == 1
        nyq = nyq_ref[...].astype(F32) * (1.0 / math.sqrt(2 * sh))
        a = _dot(t_ref[:, :sh], z_ref[:sh, :]) + jnp.where(odd, -nyq, nyq)
        b = _dot(t_ref[:, sh:], z_ref[sh:, :])
        y_ref[...] = (a - b)[:tr].astype(BF16)
        stash_ref[i] = (a + b).astype(BF16)

    @pl.when(i >= n_half)
    def _():
        y_ref[...] = _dot(p_ref[...], stash_ref[2 * n_half - 1 - i]).astype(BF16)

    mix = _dot(y_ref[...], w_ref[...]) + bo_ref[...]
    o_ref[...] = _layer_norm(DEEPNORM_ALPHA * h_ref[...] + mix, g_ref[...], b_ref[...])


def _fnet_seq(table, zf, nyq, h, w_out, b_out, ln_g, ln_b):
    bsz, s, d = h.shape
    n_half, rows, _ = table.shape
    tr = rows - HALO
    assert tr % 2 == 0
    z2 = zf.reshape(bsz, s, d)
    flip = np.zeros((tr, rows), np.float32)
    flip[np.arange(tr), tr - np.arange(tr)] = 1.0
    return pl.pallas_call(
        functools.partial(_fnet_seq_kernel, n_half=n_half),
        grid=(bsz, 2 * n_half),
        in_specs=[
            pl.BlockSpec((None, rows, s), lambda b, i: (jnp.minimum(i, n_half - 1), 0, 0)),
            pl.BlockSpec((None, s, d), lambda b, i: (b, 0, 0)),
            pl.BlockSpec((None, 1, d), lambda b, i: (b, 0, 0)),
            pl.BlockSpec((None, tr, d), lambda b, i: (b, i, 0)),
            _const_spec(flip.shape), _const_spec(w_out.shape),
            _const_spec((1, d)), _const_spec((1, d)), _const_spec((1, d)),
        ],
        out_specs=pl.BlockSpec((None, tr, d), lambda b, i: (b, i, 0)),
        out_shape=jax.ShapeDtypeStruct((bsz, s, d), F32),
        scratch_shapes=[
            pltpu.VMEM((tr, d), BF16),
            pltpu.VMEM((n_half, rows, d), BF16),
        ],
        compiler_params=_params("parallel", "arbitrary"),
        name="fnet_seq",
    )(table, z2, nyq, h, jnp.asarray(flip, dtype=BF16), w_out, b_out.reshape(1, d),
      ln_g.reshape(1, d), ln_b.reshape(1, d))


def _halo_specs(tm, d, n_rows):
    per = tm // HALO
    last = n_rows // HALO - 1
    return [
        pl.BlockSpec((HALO, d), lambda i: (jnp.maximum(i * per - 1, 0), 0)),
        pl.BlockSpec((tm, d), lambda i: (i, 0)),
        pl.BlockSpec((HALO, d), lambda i: (jnp.minimum((i + 1) * per, last), 0)),
    ]


def _assemble_interleaved(xp_ref, nat_ref, hp_ref, hm_ref, hn_ref, tiles_per_seq):
    tm, d = hm_ref.shape
    r = tm + 2 * HALO
    groups = r // 8
    pos = lax.rem(pl.program_id(0), tiles_per_seq)
    keep_prev = (pos != 0).astype(F32)
    keep_next = (pos != tiles_per_seq - 1).astype(F32)
    lane_tiles = d // V7X_LANES
    for l in range(lane_tiles):
        lanes = slice(l * V7X_LANES, (l + 1) * V7X_LANES)
        nat_ref[l, 0:HALO, :] = hp_ref[:, lanes] * keep_prev
        nat_ref[l, HALO:HALO + tm, :] = hm_ref[:, lanes]
        nat_ref[l, HALO + tm:, :] = hn_ref[:, lanes] * keep_next
    gather = lambda j: jnp.concatenate(
        [nat_ref[l, pl.ds(j, 8, stride=groups), :] for l in range(lane_tiles)], axis=1)
    for j in range(0, groups, 2):
        xp_ref[8 * j:8 * j + 16, :] = jnp.concatenate([gather(j), gather(j + 1)], axis=0).astype(BF16)


def _natural_rows(val, un_ref):
    r, c = val.shape
    groups = r // 8
    lane_tiles = c // V7X_LANES
    for j in range(groups):
        for l in range(lane_tiles):
            un_ref[l, pl.ds(j, 8, stride=groups), :] = val[8 * j:8 * j + 8, l * V7X_LANES:(l + 1) * V7X_LANES]
    return jnp.concatenate([un_ref[l, HALO:r - HALO, :] for l in range(lane_tiles)], axis=1)


def _tap(u, delta):
    r = u.shape[0]
    if delta == 0:
        return u
    if delta > 0:
        wrap = pltpu.roll(u[:8 * delta], 8 * delta - 1, 0)
        return jnp.concatenate([u[8 * delta:], wrap], axis=0)
    wrap = pltpu.roll(u[r + 8 * delta:], 1, 0)
    return jnp.concatenate([wrap, u[:r + 8 * delta]], axis=0)


def _conv_taps(u, w, bias):
    width = w.shape[0]
    half = width // 2
    out = w[half:half + 1, :] * u + bias
    for k in range(width):
        if k != half:
            out = out + w[k:k + 1, :] * _tap(u, k - half)
    return out


def _conv_ffn_kernel(hp_ref, hm_ref, hn_ref, wu_ref, cw_ref, cb_ref, wd_ref, g_ref, b_ref, o_ref,
                     xp_ref, nat_ref, ug_ref, uv_ref, act_ref, *, tiles_per_seq, n_chunks):
    cw = ug_ref.shape[2]
    d_ff = n_chunks * cw
    slots = ug_ref.shape[0]
    _assemble_interleaved(xp_ref, nat_ref, hp_ref, hm_ref, hn_ref, tiles_per_seq)

    def project(c):
        gate, val = slice(c * cw, (c + 1) * cw), slice(d_ff + c * cw, d_ff + (c + 1) * cw)
        ug_ref[c % slots] = _dot(xp_ref[...], wu_ref[:, gate])
        uv_ref[c % slots] = _dot(xp_ref[...], wu_ref[:, val])

    for c in range(min(PROJ_AHEAD, n_chunks)):
        project(c)
    for c in range(n_chunks):
        if c + PROJ_AHEAD < n_chunks:
            project(c + PROJ_AHEAD)
        gate, val = slice(c * cw, (c + 1) * cw), slice(d_ff + c * cw, d_ff + (c + 1) * cw)
        cg = _conv_taps(ug_ref[c % slots], cw_ref[:, gate], cb_ref[:, gate])
        cv = _conv_taps(uv_ref[c % slots], cw_ref[:, val], cb_ref[:, val])
        act_ref[:, gate] = (_silu(cg) * cv).astype(BF16)
    ffn = _natural_rows(_dot(act_ref[...], wd_ref[...]), nat_ref)
    o_ref[...] = _layer_norm(DEEPNORM_ALPHA * hm_ref[...] + ffn, g_ref[...], b_ref[...])


def _conv_ffn(h, seq_len, layer, w_up, conv_w, conv_b, w_down, ln_g, ln_b):
    n_rows, d = h.shape
    d_ff = w_down.shape[1]
    tm = min(TOKEN_TILE, seq_len)
    r = tm + 2 * HALO
    cw = FF_CHUNK
    nc = d_ff // cw
    assert d_ff % cw == 0 and conv_w.shape[1] // 2 <= HALO
    conv_b3 = conv_b.reshape(conv_b.shape[0], 1, 2 * d_ff)
    return pl.pallas_call(
        functools.partial(_conv_ffn_kernel, tiles_per_seq=seq_len // tm, n_chunks=nc),
        grid=(n_rows // tm,),
        in_specs=_halo_specs(tm, d, n_rows) + [
            _layer_spec(w_up, layer), _layer_spec(conv_w, layer), _layer_spec(conv_b3, layer),
            _layer_spec(w_down, layer), _const_spec((1, d)), _const_spec((1, d)),
        ],
        out_specs=pl.BlockSpec((tm, d), lambda i: (i, 0)),
        out_shape=jax.ShapeDtypeStruct((n_rows, d), F32),
        scratch_shapes=[
            pltpu.VMEM((r, d), BF16),
            pltpu.VMEM((d // V7X_LANES, r, V7X_LANES), F32),
            pltpu.VMEM((PROJ_AHEAD + 1, r, cw), F32),
            pltpu.VMEM((PROJ_AHEAD + 1, r, cw), F32),
            pltpu.VMEM((r, d_ff), BF16),
        ],
        compiler_params=_params("parallel"),
        name="conv_ffn",
    )(h, h, h, w_up, conv_w, conv_b3, w_down, ln_g.reshape(1, d), ln_b.reshape(1, d))


def _ssd_in_kernel(hp_ref, hm_ref, hn_ref, win_ref, cw_ref, cb_ref, wdt_ref, dtb_ref, alog_ref,
                   selc_ref, z_ref, xbc_ref, a3_ref, b2_ref, pt_ref, xp_ref, nat_ref, xn_ref, u_ref, un_ref,
                   *, tiles_per_seq, n_chunks):
    tm = hm_ref.shape[0]
    cw = u_ref.shape[2]
    slots = u_ref.shape[0]
    d_inner = z_ref.shape[1]
    _assemble_interleaved(xp_ref, nat_ref, hp_ref, hm_ref, hn_ref, tiles_per_seq)
    xn_ref[...] = hm_ref[...].astype(BF16)

    lane = lax.broadcasted_iota(jnp.int32, (1, V7X_LANES), 1)
    is_cum = (lane & 8) != 0
    is_bwd = (lane & 4) != 0
    row = lax.broadcasted_iota(jnp.int32, (CHUNK, CHUNK), 0)
    col = lax.broadcasted_iota(jnp.int32, (CHUNK, CHUNK), 1)
    tri = (row >= col).astype(BF16)
    n_sub = tm // CHUNK
    state = {}

    def dt_project():
        raw = _dot(xn_ref[...], wdt_ref[...]) + dtb_ref[...]
        state["dt"] = jnp.maximum(raw, 0.0) + jnp.log(1.0 + jnp.exp(-jnp.abs(raw)))
        state["dta"] = state["dt"] * -jnp.exp(alog_ref[...])

    def dt_cumsum(k):
        dk = state["dta"][k * CHUNK:(k + 1) * CHUNK, :]
        t0, t1, t2 = _split_bf16(dk, 3)
        state["cf", k] = _dot(tri, t2) + _dot(tri, t1) + _dot(tri, t0)

    def dt_pack(k):
        rows = slice(k * CHUNK, (k + 1) * CHUNK)
        dk, cf, dtk = state["dta"][rows, :], state["cf", k], state["dt"][rows, :]
        cb = cf[CHUNK - 1:CHUNK, :] - cf + dk
        cum = jnp.where(is_bwd, cb, cf)
        end = jnp.where(is_bwd, cum[0:1, :], cum[CHUNK - 1:CHUNK, :])
        pa = jnp.where(is_cum, cum, dtk)
        pb = jnp.where(is_cum, dtk * jnp.exp(end - cum), jnp.exp(cum))
        a3_ref[rows, :] = _dot(jnp.concatenate(_split_bf16(pa, 3), axis=1), selc_ref[...]).astype(BF16)
        for j, term in enumerate(_split_bf16(pb, 2)):
            b2_ref[rows, j * V7X_LANES:(j + 1) * V7X_LANES] = term
        pt_ref[:, rows] = pa.T

    stages = ([dt_project] + [functools.partial(dt_cumsum, k) for k in range(n_sub)]
              + [functools.partial(dt_pack, k) for k in range(n_sub)])

    def project(c):
        u_ref[c % slots] = _dot(xp_ref[...], win_ref[:, d_inner + c * cw:d_inner + (c + 1) * cw])

    z_steps = d_inner // cw
    for c in range(min(PROJ_AHEAD, n_chunks)):
        project(c)
    for c in range(n_chunks):
        if c + PROJ_AHEAD < n_chunks:
            project(c + PROJ_AHEAD)
        if c < len(stages):
            stages[c]()
        for s in range(c * z_steps // n_chunks, (c + 1) * z_steps // n_chunks):
            zc = slice(s * cw, (s + 1) * cw)
            z_ref[:, zc] = _dot(xn_ref[...], win_ref[:, zc]).astype(BF16)
        cols = slice(c * cw, (c + 1) * cw)
        v = _silu(_conv_taps(u_ref[c % slots], cw_ref[:, cols], cb_ref[:, cols]))
        xbc_ref[:, cols] = _natural_rows(v, un_ref.at[c % 2]).astype(BF16)
    for stage in stages[n_chunks:]:
        stage()


def _ssd_in(h, seq_len, w_in, d_inner, conv_w, conv_b, w_dt, dt_bias, a_log):
    n_rows, d = h.shape
    width, conv_dim = conv_w.shape
    tm = min(TOKEN_TILE, seq_len)
    r = tm + 2 * HALO
    cw = XBC_CHUNK
    nc = conv_dim // cw
    assert conv_dim % cw == 0 and d_inner % V7X_LANES == 0 and width // 2 <= HALO
    selc = jnp.asarray(_cumsum_compaction_matrix(), dtype=BF16)
    return pl.pallas_call(
        functools.partial(_ssd_in_kernel, tiles_per_seq=seq_len // tm, n_chunks=nc),
        grid=(n_rows // tm,),
        in_specs=_halo_specs(tm, d, n_rows) + [
            _const_spec(w_in.shape), _const_spec(conv_w.shape), _const_spec((1, conv_dim)),
            _const_spec(w_dt.shape), _const_spec((1, V7X_LANES)), _const_spec((1, V7X_LANES)),
            _const_spec(selc.shape),
        ],
        out_specs=[
            pl.BlockSpec((tm, d_inner), lambda i: (i, 0)),
            pl.BlockSpec((tm, conv_dim), lambda i: (i, 0)),
            pl.BlockSpec((tm, 2 * V7X_LANES), lambda i: (i, 0)),
            pl.BlockSpec((tm, 2 * V7X_LANES), lambda i: (i, 0)),
            pl.BlockSpec((V7X_LANES, tm), lambda i: (0, i)),
        ],
        out_shape=[
            jax.ShapeDtypeStruct((n_rows, d_inner), BF16),
            jax.ShapeDtypeStruct((n_rows, conv_dim), BF16),
            jax.ShapeDtypeStruct((n_rows, 2 * V7X_LANES), BF16),
            jax.ShapeDtypeStruct((n_rows, 2 * V7X_LANES), BF16),
            jax.ShapeDtypeStruct((V7X_LANES, n_rows), F32),
        ],
        scratch_shapes=[
            pltpu.VMEM((r, d), BF16),
            pltpu.VMEM((d // V7X_LANES, r, V7X_LANES), F32),
            pltpu.VMEM((tm, d), BF16),
            pltpu.VMEM((PROJ_AHEAD + 1, r, cw), F32),
            pltpu.VMEM((2, cw // V7X_LANES, r, V7X_LANES), F32),
        ],
        compiler_params=_params("parallel"),
        name="ssd_in",
    )(h, h, h, w_in, conv_w, conv_b.reshape(1, conv_dim), w_dt, dt_bias, a_log, selc)


def _ssd_scan_kernel(x_ref, b_ref, c_ref, z_ref, a3_ref, b2_ref, pt_ref, dsk_ref, ng_ref, o_ref,
                     y_ref, sb_ref, xbd_all_ref, sel_l_ref, sel_e_ref, sel_b_ref, *, n_chunks, heads, unroll):
    q = CHUNK
    p = HEAD_DIM
    w = heads * p
    g = pl.program_id(1)
    log_q, log_p = q.bit_length() - 1, p.bit_length() - 1

    src = lax.broadcasted_iota(jnp.int32, sel_l_ref.shape, 0)
    blk = lax.broadcasted_iota(jnp.int32, sel_l_ref.shape, 1) >> log_q
    hd = 4 * (blk & 1) + (blk >> 1)
    sel_l_ref[...] = (((src >> 5) == g) & ((src & 7) == hd) & (((src >> 3) & 3) != 3)).astype(BF16)

    def selection(ref, source_offset):
        src = lax.broadcasted_iota(jnp.int32, ref.shape, 0) & (V7X_LANES - 1)
        out = lax.broadcasted_iota(jnp.int32, ref.shape, 1)
        ref[...] = (src == PACK * g + source_offset(out)).astype(BF16)

    selection(sel_e_ref, lambda j: jnp.where(j < w, j >> log_p, 8 + ((j - w) >> log_p)))
    selection(sel_b_ref, lambda j: 4 + (j >> log_p))

    @pl.when((pl.program_id(0) == 0) & (g == 0))
    def _():
        xbd_all_ref[...] = jnp.zeros_like(xbd_all_ref)

    row = lax.broadcasted_iota(jnp.int32, (q, q), 0)
    col = lax.broadcasted_iota(jnp.int32, (q, q), 1)
    lower = row >= col
    upper = row <= col
    neg_inf = jnp.float32(-jnp.inf)

    def pass1(i, hf):
        chunks = range(unroll[0])
        rows = [pl.ds(pl.multiple_of((i * unroll[0] + u) * q, q), q) for u in chunks]
        xc = [x_ref[rw, :] for rw in rows]
        bc = [b_ref[rw, :] for rw in rows]
        cc = [c_ref[rw, :] for rw in rows]
        pt = [pt_ref[:, rw] for rw in rows]
        trip = pl.ds(pl.multiple_of(i * unroll[0] * q, q), unroll[0] * q)
        a3_trip = a3_ref[trip, :]
        b2_trip = b2_ref[trip, :]
        cbm = [lax.dot_general(cc[u], bc[u], (((1,), (1,)), ((), ())), preferred_element_type=F32)
               for u in chunks]
        ms = [[] for _ in chunks]
        for r in range(heads):
            cum_trip = _dot(a3_trip, sel_l_ref[:, 2 * r * q:2 * (r + 1) * q])
            for u in chunks:
                cum_col = cum_trip[u * q:(u + 1) * q]
                seg_f = cum_col[:, :q] - pt[u][8 + r:9 + r, :]
                seg_b = cum_col[:, q:] - pt[u][12 + r:13 + r, :]
                lf = jnp.exp(jnp.where(lower, seg_f, neg_inf)) * pt[u][r:r + 1, :]
                lb = jnp.exp(jnp.where(upper, seg_b, neg_inf)) * pt[u][4 + r:5 + r, :]
                ms[u].append((cbm[u] * (lf + lb)).astype(BF16))
                xbd_all_ref[u, r * q:(r + 1) * q, r * p:(r + 1) * p] = xc[u][:, r * p:(r + 1) * p]
        yd = [_dot(jnp.concatenate(ms[u], axis=1), xbd_all_ref[u]) for u in chunks]
        xf = [xc[u].astype(F32) for u in chunks]
        ex_trip = _dot(b2_trip, sel_e_ref[...])
        ex = [ex_trip[u * q:(u + 1) * q] for u in chunks]
        xdec = [jnp.concatenate([(xf[u] * ex[u][:, w:2 * w]).astype(BF16),
                                 (xf[u] * ex[u][:, 2 * w:]).astype(BF16)], axis=1)
                for u in chunks]
        st = [lax.dot_general(bc[u], xdec[u], (((0,), (0,)), ((), ())), preferred_element_type=F32)
              for u in chunks]
        ef = [ex[u][:, :w] for u in chunks]
        for u in chunks:
            y_ref[rows[u], :] = yd[u] + _dot(cc[u], hf.astype(BF16)) * ef[u] + dsk_ref[...] * xf[u]
            sb_ref[i * unroll[0] + u] = st[u][:, w:]
            hf = hf * ef[u][q - 1:q, :] + st[u][:, :w]
        return hf

    def pass2(k, hb):
        c = n_chunks - 1 - k
        rows = pl.ds(pl.multiple_of(c * q, q), q)
        eb = _dot(b2_ref[rows, :], sel_b_ref[...])
        y = y_ref[rows, :] + _dot(c_ref[rows, :], hb.astype(BF16)) * eb
        y = y * _silu(z_ref[rows, :].astype(F32))
        y = y * lax.rsqrt(jnp.mean(y * y, axis=-1, keepdims=True) + LN_EPS)
        o_ref[rows, :] = (y * ng_ref[...]).astype(BF16)
        return hb * eb[0:1, :] + sb_ref[c]

    zero = jnp.zeros((D_STATE, w), F32)
    lax.fori_loop(0, n_chunks // unroll[0], pass1, zero)
    lax.fori_loop(0, n_chunks, pass2, zero, unroll=unroll[1])


def _ssd_scan(z, xbc, a3, b2, pack_t, d_skip, norm_g, bsz, seq_len):
    d_inner = z.shape[1]
    groups = SSM_GROUPS
    w = d_inner // groups
    heads = w // HEAD_DIM
    n = D_STATE
    nc = seq_len // CHUNK
    assert nc % SCAN_UNROLL[0] == 0
    z3 = z.reshape(bsz, seq_len, d_inner)
    xbc3 = xbc.reshape(bsz, seq_len, xbc.shape[1])
    a33 = a3.reshape(bsz, seq_len, a3.shape[1])
    b23 = b2.reshape(bsz, seq_len, b2.shape[1])
    b_off = d_inner // n
    c_off = b_off + groups
    return pl.pallas_call(
        functools.partial(_ssd_scan_kernel, n_chunks=nc, heads=heads, unroll=SCAN_UNROLL),
        grid=(bsz, groups),
        in_specs=[
            pl.BlockSpec((None, seq_len, w), lambda b, g: (b, 0, g)),
            pl.BlockSpec((None, seq_len, n), lambda b, g: (b, 0, b_off + g)),
            pl.BlockSpec((None, seq_len, n), lambda b, g: (b, 0, c_off + g)),
            pl.BlockSpec((None, seq_len, w), lambda b, g: (b, 0, g)),
            pl.BlockSpec((None, seq_len, a3.shape[1]), lambda b, g: (b, 0, 0)),
            pl.BlockSpec((None, seq_len, b2.shape[1]), lambda b, g: (b, 0, 0)),
            pl.BlockSpec((PACK, seq_len), lambda b, g: (g, b)),
            pl.BlockSpec((1, w), lambda b, g: (0, g)),
            pl.BlockSpec((1, w), lambda b, g: (0, g)),
        ],
        out_specs=pl.BlockSpec((None, seq_len, w), lambda b, g: (b, 0, g)),
        out_shape=jax.ShapeDtypeStruct((bsz, seq_len, d_inner), BF16),
        scratch_shapes=[
            pltpu.VMEM((seq_len, w), F32),
            pltpu.VMEM((nc, n, w), F32),
            pltpu.VMEM((SCAN_UNROLL[0], heads * CHUNK, w), BF16),
            pltpu.VMEM((a3.shape[1], 2 * heads * CHUNK), BF16),
            pltpu.VMEM((b2.shape[1], 3 * w), BF16),
            pltpu.VMEM((b2.shape[1], w), BF16),
        ],
        compiler_params=_params("arbitrary", "arbitrary"),
        name="ssd_scan",
    )(xbc3, xbc3, xbc3, z3, a33, b23, pack_t, d_skip, norm_g.reshape(1, d_inner))


def _proj_ln_kernel(y_ref, h_ref, w_ref, g_ref, b_ref, o_ref):
    mix = _dot(y_ref[...], w_ref[...])
    o_ref[...] = _layer_norm(DEEPNORM_ALPHA * h_ref[...] + mix, g_ref[...], b_ref[...])


def _proj_ln(y, h, w, ln_g, ln_b):
    n_rows, d = h.shape
    k = y.shape[1]
    tm = min(PLAIN_TILE, n_rows)
    return pl.pallas_call(
        _proj_ln_kernel,
        grid=(n_rows // tm,),
        in_specs=[
            pl.BlockSpec((tm, k), lambda i: (i, 0)),
            pl.BlockSpec((tm, d), lambda i: (i, 0)),
            _const_spec(w.shape), _const_spec((1, d)), _const_spec((1, d)),
        ],
        out_specs=pl.BlockSpec((tm, d), lambda i: (i, 0)),
        out_shape=jax.ShapeDtypeStruct((n_rows, d), F32),
        compiler_params=_params("parallel"),
        name="proj_ln",
    )(y, h, w.astype(BF16), ln_g.reshape(1, d), ln_b.reshape(1, d))


def _channel_dft_table(gd):
    k = np.arange(gd)
    ang = 2.0 * np.pi * ((k[:, None] * k[None, :]) % gd) / gd
    t = np.concatenate([np.cos(ang), np.sin(ang)], axis=1) / math.sqrt(gd)
    return jnp.asarray(t, dtype=BF16)


def _seq_dft_table(s):
    tr = min(DFT_ROW_TILE, s // 2)
    n_half = s // 2 // tr
    n = jnp.arange(s // 2, dtype=jnp.int32)
    row = jnp.arange(tr + HALO, dtype=jnp.int32)
    tile = jnp.arange(n_half, dtype=jnp.int32) * tr
    unit = 2.0 * math.pi / s
    ang_r = unit * ((row[:, None] * n[None, :]) % s).astype(F32)
    ang_t = unit * ((tile[:, None] * n[None, :]) % s).astype(F32)
    scale = 1.0 / math.sqrt(s)
    cr, sr = jnp.cos(ang_r)[None] * scale, jnp.sin(ang_r)[None] * scale
    ct, st = jnp.cos(ang_t)[:, None], jnp.sin(ang_t)[:, None]
    return jnp.concatenate([ct * cr - st * sr, st * cr + ct * sr], axis=2).astype(BF16)


def _cumsum_compaction_matrix():
    m = np.zeros((3 * V7X_LANES, 2 * V7X_LANES), np.float32)
    for g in range(SSM_GROUPS):
        for t in range(3):
            for hd in range(8):
                m[t * V7X_LANES + PACK * g + 8 + hd, 32 * g + 8 * t + hd] = 1.0
    return m


def _packed_lane_heads():
    lane = np.arange(V7X_LANES)
    g, d, r = lane // PACK, (lane // 4) % 2, lane % 4
    return d, g * 4 + r


def kernel(x, emb_ln_g, emb_ln_b, fn_w_in, fn_b_in, fn_w_out, fn_b_out, ssd_w_in, ssd_conv_w, ssd_conv_b, ssd_a_log_fwd, ssd_a_log_bwd, ssd_dt_bias_fwd, ssd_dt_bias_bwd, ssd_d, ssd_norm_g, ssd_w_out, ln_tok_g, ln_tok_b, ff_w_up, ff_conv_w, ff_conv_b, ff_w_down, ln_ffn_g, ln_ffn_b):
    bsz, s, d = x.shape
    n_rows = bsz * s
    d_inner = SSM_EXPAND * d
    n_heads = d_inner // HEAD_DIM
    conv_dim = d_inner + 2 * SSM_GROUPS * D_STATE
    assert s % TOKEN_TILE == 0 and s % CHUNK == 0 and d % (FN_GROUPS * V7X_LANES) == 0
    assert n_heads // SSM_GROUPS == 4 and SSM_GROUPS * PACK == V7X_LANES

    ff_w_up_b, ff_w_down_b = ff_w_up.astype(BF16), ff_w_down.astype(BF16)
    h = None
    for i in range(DEPTH):
        j = i // N_MIXERS
        if i % N_MIXERS == 0:
            cs = _channel_dft_table(d // FN_GROUPS)
            table = _seq_dft_table(s)
            if h is None:
                h0, z = _fnet_in(x, emb_ln_g, emb_ln_b, fn_w_in[j].astype(BF16), fn_b_in[j], cs)
            else:
                raise NotImplementedError("only the first layer uses the Fourier mixer at this depth")
            nyq = z[:, 0, s // 2:s // 2 + 1, :]
            h = _fnet_seq(table, _fnet_fold(z), nyq, h0, fn_w_out[j].astype(BF16), fn_b_out[j],
                          ln_tok_g[i], ln_tok_b[i]).reshape(n_rows, d)
        else:
            w_in = ssd_w_in[j].astype(BF16)
            lane_dir, lane_head = _packed_lane_heads()
            w_dt = ssd_w_in[j][:, d_inner + conv_dim:][:, lane_dir * n_heads + lane_head].astype(BF16)
            pick = lambda f, b: jnp.where(jnp.asarray(lane_dir == 1), b[lane_head], f[lane_head]).reshape(1, V7X_LANES)
            z, xbc, a3, b2, pack_t = _ssd_in(
                h, s, w_in, d_inner, ssd_conv_w[j], ssd_conv_b[j],
                w_dt, pick(ssd_dt_bias_fwd[j], ssd_dt_bias_bwd[j]), pick(ssd_a_log_fwd[j], ssd_a_log_bwd[j]))
            d_skip = jnp.repeat(ssd_d[j], HEAD_DIM).reshape(1, d_inner)
            yn = _ssd_scan(z, xbc, a3, b2, pack_t, d_skip, ssd_norm_g[j], bsz, s)
            h = _proj_ln(yn.reshape(n_rows, d_inner), h, ssd_w_out[j], ln_tok_g[i], ln_tok_b[i])
        h = _conv_ffn(h, s, i, ff_w_up_b, ff_conv_w, ff_conv_b, ff_w_down_b, ln_ffn_g[i], ln_ffn_b[i])
    return h.reshape(bsz, s, d)
```

```python
import functools
import math

import numpy as np
import jax
import jax.numpy as jnp
from jax import lax
from jax.experimental import pallas as pl
from jax.experimental.pallas import tpu as pltpu

F32 = jnp.float32
BF16 = jnp.bfloat16

LN_EPS = 1e-5
DEPTH = 2
N_MIXERS = 2
DEEPNORM_ALPHA = (2.0 * DEPTH) ** 0.25
FN_GROUPS = 8
HEAD_DIM = 64
D_STATE = 128
SSM_GROUPS = 8
SSM_EXPAND = 2
CHUNK = 128

V7X_LANES = 128
V7X_BF16_SUBLANES = 16
V7X_VMEM_LIMIT_BYTES = 56 * 1024 * 1024

TOKEN_TILE = 512
PLAIN_TILE = 1024
HALO = V7X_BF16_SUBLANES
FF_CHUNK = 256
XBC_CHUNK = 256
DFT_ROW_TILE = 512
PACK = 16
PROJ_AHEAD = 2
SCAN_UNROLL = (16, 8)


def _layer_norm(v, g, b):
    mu = jnp.mean(v, axis=-1, keepdims=True)
    vc = v - mu
    var = jnp.mean(vc * vc, axis=-1, keepdims=True)
    return vc * lax.rsqrt(var + LN_EPS) * g + b


def _silu(v):
    hv = 0.5 * v
    return hv + hv * jnp.tanh(hv)


def _dot(a, b):
    return jnp.dot(a, b, preferred_element_type=F32)


def _split_bf16(v, terms):
    out = []
    for _ in range(terms):
        t = v.astype(BF16)
        out.append(t)
        v = v - t.astype(F32)
    return out


def _params(*semantics):
    return pltpu.CompilerParams(dimension_semantics=semantics,
                                vmem_limit_bytes=V7X_VMEM_LIMIT_BYTES)


def _const_spec(shape):
    zeros = (0,) * len(shape)
    return pl.BlockSpec(shape, lambda *_: zeros, pipeline_mode=pl.Buffered(1))


def _layer_spec(stacked, layer):
    zeros = (0,) * (stacked.ndim - 1)
    return pl.BlockSpec((None,) + stacked.shape[1:], lambda *_: (layer,) + zeros, pipeline_mode=pl.Buffered(1))


def _fnet_in_kernel(x_ref, g_ref, b_ref, w_ref, bi_ref, cs_ref, h_ref, z_ref, *, groups):
    h = _layer_norm(x_ref[...], g_ref[...], b_ref[...])
    h_ref[...] = h
    u = (_dot(h.astype(BF16), w_ref[...]) + bi_ref[...]).astype(BF16)
    gd = u.shape[1] // groups
    for g in range(groups):
        r = _dot(u[:, g * gd:(g + 1) * gd], cs_ref[...])
        z_ref[0, :, g * gd:(g + 1) * gd] = r[:, :gd].astype(BF16)
        z_ref[1, :, g * gd:(g + 1) * gd] = r[:, gd:].astype(BF16)


def _fnet_in(x, ln_g, ln_b, w_in, b_in, cs):
    bsz, s, d = x.shape
    tm = min(PLAIN_TILE, s)
    nt = s // tm
    return pl.pallas_call(
        functools.partial(_fnet_in_kernel, groups=FN_GROUPS),
        grid=(bsz, nt),
        in_specs=[
            pl.BlockSpec((None, tm, d), lambda b, i: (b, i, 0)),
            _const_spec((1, d)), _const_spec((1, d)),
            _const_spec(w_in.shape), _const_spec((1, d)), _const_spec(cs.shape),
        ],
        out_specs=[
            pl.BlockSpec((None, tm, d), lambda b, i: (b, i, 0)),
            pl.BlockSpec((None, 2, tm, d), lambda b, i: (b, 0, i, 0)),
        ],
        out_shape=[jax.ShapeDtypeStruct((bsz, s, d), F32),
                   jax.ShapeDtypeStruct((bsz, 2, s, d), BF16)],
        compiler_params=_params("parallel", "parallel"),
        name="fnet_in",
    )(x, ln_g.reshape(1, d), ln_b.reshape(1, d), w_in, b_in.reshape(1, d), cs)


def _fnet_fold_kernel(zt_ref, zm_ref, ze_ref, p_ref, o_ref):
    keep = (pl.program_id(1) != 0).astype(F32)
    for part, sign in ((0, 1.0), (1, -1.0)):
        extra = (ze_ref[part].astype(F32) * keep).astype(BF16)
        mirrored = _dot(p_ref[...], jnp.concatenate([zm_ref[part], extra], axis=0))
        o_ref[part] = (zt_ref[part].astype(F32) + sign * mirrored).astype(BF16)


def _fnet_fold(z):
    bsz, _, s, d = z.shape
    tf = min(DFT_ROW_TILE, s // 2)
    nt = s // tf
    flip = np.zeros((tf, tf + HALO), np.float32)
    flip[np.arange(tf), tf - np.arange(tf)] = 1.0
    per = tf // HALO
    last = s // HALO - 1
    return pl.pallas_call(
        _fnet_fold_kernel,
        grid=(bsz, nt // 2),
        in_specs=[
            pl.BlockSpec((None, 2, tf, d), lambda b, j: (b, 0, j, 0)),
            pl.BlockSpec((None, 2, tf, d), lambda b, j: (b, 0, nt - 1 - j, 0)),
            pl.BlockSpec((None, 2, HALO, d), lambda b, j: (b, 0, jnp.minimum((nt - j) * per, last), 0)),
            _const_spec(flip.shape),
        ],
        out_specs=pl.BlockSpec((None, 2, tf, d), lambda b, j: (b, 0, j, 0)),
        out_shape=jax.ShapeDtypeStruct((bsz, 2, s // 2, d), BF16),
        compiler_params=_params("parallel", "parallel"),
        name="fnet_fold",
    )(z, z, z, jnp.asarray(flip, dtype=BF16))


def _fnet_seq_kernel(t_ref, z_ref, nyq_ref, h_ref, p_ref, w_ref, bo_ref, g_ref, b_ref, o_ref, y_ref, stash_ref,
                     *, n_half):
    i = pl.program_id(1)
    tr = y_ref.shape[0]
    sh = z_ref.shape[0] // 2

    @pl.when(i < n_half)
    def _():
        rows = t_ref.shape[0]
        odd = (lax.broadcasted_iota(jnp.int32, (rows, 1), 0) & 1) == 1
        nyq = nyq_ref[...].astype(F32) * (1.0 / math.sqrt(2 * sh))
        a = _dot(t_ref[:, :sh], z_ref[:sh, :]) + jnp.where(odd, -nyq, nyq)
        b = _dot(t_ref[:, sh:], z_ref[sh:, :])
        y_ref[...] = (a - b)[:tr].astype(BF16)
        stash_ref[i] = (a + b).astype(BF16)

    @pl.when(i >= n_half)
    def _():
        y_ref[...] = _dot(p_ref[...], stash_ref[2 * n_half - 1 - i]).astype(BF16)

    mix = _dot(y_ref[...], w_ref[...]) + bo_ref[...]
    o_ref[...] = _layer_norm(DEEPNORM_ALPHA * h_ref[...] + mix, g_ref[...], b_ref[...])


def _fnet_seq(table, zf, nyq, h, w_out, b_out, ln_g, ln_b):
    bsz, s, d = h.shape
    n_half, rows, _ = table.shape
    tr = rows - HALO
    assert tr % 2 == 0
    z2 = zf.reshape(bsz, s, d)
    flip = np.zeros((tr, rows), np.float32)
    flip[np.arange(tr), tr - np.arange(tr)] = 1.0
    return pl.pallas_call(
        functools.partial(_fnet_seq_kernel, n_half=n_half),
        grid=(bsz, 2 * n_half),
        in_specs=[
            pl.BlockSpec((None, rows, s), lambda b, i: (jnp.minimum(i, n_half - 1), 0, 0)),
            pl.BlockSpec((None, s, d), lambda b, i: (b, 0, 0)),
            pl.BlockSpec((None, 1, d), lambda b, i: (b, 0, 0)),
            pl.BlockSpec((None, tr, d), lambda b, i: (b, i, 0)),
            _const_spec(flip.shape), _const_spec(w_out.shape),
            _const_spec((1, d)), _const_spec((1, d)), _const_spec((1, d)),
        ],
        out_specs=pl.BlockSpec((None, tr, d), lambda b, i: (b, i, 0)),
        out_shape=jax.ShapeDtypeStruct((bsz, s, d), F32),
        scratch_shapes=[
            pltpu.VMEM((tr, d), BF16),
            pltpu.VMEM((n_half, rows, d), BF16),
        ],
        compiler_params=_params("parallel", "arbitrary"),
        name="fnet_seq",
    )(table, z2, nyq, h, jnp.asarray(flip, dtype=BF16), w_out, b_out.reshape(1, d),
      ln_g.reshape(1, d), ln_b.reshape(1, d))


def _halo_specs(tm, d, n_rows):
    per = tm // HALO
    last = n_rows // HALO - 1
    return [
        pl.BlockSpec((HALO, d), lambda i: (jnp.maximum(i * per - 1, 0), 0)),
        pl.BlockSpec((tm, d), lambda i: (i, 0)),
        pl.BlockSpec((HALO, d), lambda i: (jnp.minimum((i + 1) * per, last), 0)),
    ]


def _assemble_interleaved(xp_ref, nat_ref, hp_ref, hm_ref, hn_ref, tiles_per_seq):
    tm, d = hm_ref.shape
    r = tm + 2 * HALO
    groups = r // 8
    pos = lax.rem(pl.program_id(0), tiles_per_seq)
    keep_prev = (pos != 0).astype(F32)
    keep_next = (pos != tiles_per_seq - 1).astype(F32)
    lane_tiles = d // V7X_LANES
    for l in range(lane_tiles):
        lanes = slice(l * V7X_LANES, (l + 1) * V7X_LANES)
        nat_ref[l, 0:HALO, :] = hp_ref[:, lanes] * keep_prev
        nat_ref[l, HALO:HALO + tm, :] = hm_ref[:, lanes]
        nat_ref[l, HALO + tm:, :] = hn_ref[:, lanes] * keep_next
    gather = lambda j: jnp.concatenate(
        [nat_ref[l, pl.ds(j, 8, stride=groups), :] for l in range(lane_tiles)], axis=1)
    for j in range(0, groups, 2):
        xp_ref[8 * j:8 * j + 16, :] = jnp.concatenate([gather(j), gather(j + 1)], axis=0).astype(BF16)


def _natural_rows(val, un_ref):
    r, c = val.shape
    groups = r // 8
    lane_tiles = c // V7X_LANES
    for j in range(groups):
        for l in range(lane_tiles):
            un_ref[l, pl.ds(j, 8, stride=groups), :] = val[8 * j:8 * j + 8, l * V7X_LANES:(l + 1) * V7X_LANES]
    return jnp.concatenate([un_ref[l, HALO:r - HALO, :] for l in range(lane_tiles)], axis=1)


def _tap(u, delta):
    r = u.shape[0]
    if delta == 0:
        return u
    if delta > 0:
        wrap = pltpu.roll(u[:8 * delta], 8 * delta - 1, 0)
        return jnp.concatenate([u[8 * delta:], wrap], axis=0)
    wrap = pltpu.roll(u[r + 8 * delta:], 1, 0)
    return jnp.concatenate([wrap, u[:r + 8 * delta]], axis=0)


def _conv_taps(u, w, bias):
    width = w.shape[0]
    half = width // 2
    out = w[half:half + 1, :] * u + bias
    for k in range(width):
        if k != half:
            out = out + w[k:k + 1, :] * _tap(u, k - half)
    return out


def _conv_ffn_kernel(hp_ref, hm_ref, hn_ref, wu_ref, cw_ref, cb_ref, wd_ref, g_ref, b_ref, o_ref,
                     xp_ref, nat_ref, ug_ref, uv_ref, act_ref, *, tiles_per_seq, n_chunks):
    cw = ug_ref.shape[2]
    d_ff = n_chunks * cw
    slots = ug_ref.shape[0]
    _assemble_interleaved(xp_ref, nat_ref, hp_ref, hm_ref, hn_ref, tiles_per_seq)

    def project(c):
        gate, val = slice(c * cw, (c + 1) * cw), slice(d_ff + c * cw, d_ff + (c + 1) * cw)
        ug_ref[c % slots] = _dot(xp_ref[...], wu_ref[:, gate])
        uv_ref[c % slots] = _dot(xp_ref[...], wu_ref[:, val])

    for c in range(min(PROJ_AHEAD, n_chunks)):
        project(c)
    for c in range(n_chunks):
        if c + PROJ_AHEAD < n_chunks:
            project(c + PROJ_AHEAD)
        gate, val = slice(c * cw, (c + 1) * cw), slice(d_ff + c * cw, d_ff + (c + 1) * cw)
        cg = _conv_taps(ug_ref[c % slots], cw_ref[:, gate], cb_ref[:, gate])
        cv = _conv_taps(uv_ref[c % slots], cw_ref[:, val], cb_ref[:, val])
        act_ref[:, gate] = (_silu(cg) * cv).astype(BF16)
    ffn = _natural_rows(_dot(act_ref[...], wd_ref[...]), nat_ref)
    o_ref[...] = _layer_norm(DEEPNORM_ALPHA * hm_ref[...] + ffn, g_ref[...], b_ref[...])


def _conv_ffn(h, seq_len, layer, w_up, conv_w, conv_b, w_down, ln_g, ln_b):
    n_rows, d = h.shape
    d_ff = w_down.shape[1]
    tm = min(TOKEN_TILE, seq_len)
    r = tm + 2 * HALO
    cw = FF_CHUNK
    nc = d_ff // cw
    assert d_ff % cw == 0 and conv_w.shape[1] // 2 <= HALO
    conv_b3 = conv_b.reshape(conv_b.shape[0], 1, 2 * d_ff)
    return pl.pallas_call(
        functools.partial(_conv_ffn_kernel, tiles_per_seq=seq_len // tm, n_chunks=nc),
        grid=(n_rows // tm,),
        in_specs=_halo_specs(tm, d, n_rows) + [
            _layer_spec(w_up, layer), _layer_spec(conv_w, layer), _layer_spec(conv_b3, layer),
            _layer_spec(w_down, layer), _const_spec((1, d)), _const_spec((1, d)),
        ],
        out_specs=pl.BlockSpec((tm, d), lambda i: (i, 0)),
        out_shape=jax.ShapeDtypeStruct((n_rows, d), F32),
        scratch_shapes=[
            pltpu.VMEM((r, d), BF16),
            pltpu.VMEM((d // V7X_LANES, r, V7X_LANES), F32),
            pltpu.VMEM((PROJ_AHEAD + 1, r, cw), F32),
            pltpu.VMEM((PROJ_AHEAD + 1, r, cw), F32),
            pltpu.VMEM((r, d_ff), BF16),
        ],
        compiler_params=_params("parallel"),
        name="conv_ffn",
    )(h, h, h, w_up, conv_w, conv_b3, w_down, ln_g.reshape(1, d), ln_b.reshape(1, d))


def _ssd_in_kernel(hp_ref, hm_ref, hn_ref, win_ref, cw_ref, cb_ref, wdt_ref, dtb_ref, alog_ref,
                   selc_ref, z_ref, xbc_ref, a3_ref, b2_ref, pt_ref, xp_ref, nat_ref, xn_ref, u_ref, un_ref,
                   *, tiles_per_seq, n_chunks):
    tm = hm_ref.shape[0]
    cw = u_ref.shape[2]
    slots = u_ref.shape[0]
    d_inner = z_ref.shape[1]
    _assemble_interleaved(xp_ref, nat_ref, hp_ref, hm_ref, hn_ref, tiles_per_seq)
    xn_ref[...] = hm_ref[...].astype(BF16)

    lane = lax.broadcasted_iota(jnp.int32, (1, V7X_LANES), 1)
    is_cum = (lane & 8) != 0
    is_bwd = (lane & 4) != 0
    row = lax.broadcasted_iota(jnp.int32, (CHUNK, CHUNK), 0)
    col = lax.broadcasted_iota(jnp.int32, (CHUNK, CHUNK), 1)
    tri = (row >= col).astype(BF16)
    n_sub = tm // CHUNK
    state = {}

    def dt_project():
        raw = _dot(xn_ref[...], wdt_ref[...]) + dtb_ref[...]
        state["dt"] = jnp.maximum(raw, 0.0) + jnp.log(1.0 + jnp.exp(-jnp.abs(raw)))
        state["dta"] = state["dt"] * -jnp.exp(alog_ref[...])

    def dt_cumsum(k):
        dk = state["dta"][k * CHUNK:(k + 1) * CHUNK, :]
        t0, t1, t2 = _split_bf16(dk, 3)
        state["cf", k] = _dot(tri, t2) + _dot(tri, t1) + _dot(tri, t0)

    def dt_pack(k):
        rows = slice(k * CHUNK, (k + 1) * CHUNK)
        dk, cf, dtk = state["dta"][rows, :], state["cf", k], state["dt"][rows, :]
        cb = cf[CHUNK - 1:CHUNK, :] - cf + dk
        cum = jnp.where(is_bwd, cb, cf)
        end = jnp.where(is_bwd, cum[0:1, :], cum[CHUNK - 1:CHUNK, :])
        pa = jnp.where(is_cum, cum, dtk)
        pb = jnp.where(is_cum, dtk * jnp.exp(end - cum), jnp.exp(cum))
        a3_ref[rows, :] = _dot(jnp.concatenate(_split_bf16(pa, 3), axis=1), selc_ref[...]).astype(BF16)
        for j, term in enumerate(_split_bf16(pb, 2)):
            b2_ref[rows, j * V7X_LANES:(j + 1) * V7X_LANES] = term
        pt_ref[:, rows] = pa.T

    stages = ([dt_project] + [functools.partial(dt_cumsum, k) for k in range(n_sub)]
              + [functools.partial(dt_pack, k) for k in range(n_sub)])

    def project(c):
        u_ref[c % slots] = _dot(xp_ref[...], win_ref[:, d_inner + c * cw:d_inner + (c + 1) * cw])

    z_steps = d_inner // cw
    for c in range(min(PROJ_AHEAD, n_chunks)):
        project(c)
    for c in range(n_chunks):
        if c + PROJ_AHEAD < n_chunks:
            project(c + PROJ_AHEAD)
        if c < len(stages):
            stages[c]()
        for s in range(c * z_steps // n_chunks, (c + 1) * z_steps // n_chunks):
            zc = slice(s * cw, (s + 1) * cw)
            z_ref[:, zc] = _dot(xn_ref[...], win_ref[:, zc]).astype(BF16)
        cols = slice(c * cw, (c + 1) * cw)
        v = _silu(_conv_taps(u_ref[c % slots], cw_ref[:, cols], cb_ref[:, cols]))
        xbc_ref[:, cols] = _natural_rows(v, un_ref.at[c % 2]).astype(BF16)
    for stage in stages[n_chunks:]:
        stage()


def _ssd_in(h, seq_len, w_in, d_inner, conv_w, conv_b, w_dt, dt_bias, a_log):
    n_rows, d = h.shape
    width, conv_dim = conv_w.shape
    tm = min(TOKEN_TILE, seq_len)
    r = tm + 2 * HALO
    cw = XBC_CHUNK
    nc = conv_dim // cw
    assert conv_dim % cw == 0 and d_inner % V7X_LANES == 0 and width // 2 <= HALO
    selc = jnp.asarray(_cumsum_compaction_matrix(), dtype=BF16)
    return pl.pallas_call(
        functools.partial(_ssd_in_kernel, tiles_per_seq=seq_len // tm, n_chunks=nc),
        grid=(n_rows // tm,),
        in_specs=_halo_specs(tm, d, n_rows) + [
            _const_spec(w_in.shape), _const_spec(conv_w.shape), _const_spec((1, conv_dim)),
            _const_spec(w_dt.shape), _const_spec((1, V7X_LANES)), _const_spec((1, V7X_LANES)),
            _const_spec(selc.shape),
        ],
        out_specs=[
            pl.BlockSpec((tm, d_inner), lambda i: (i, 0)),
            pl.BlockSpec((tm, conv_dim), lambda i: (i, 0)),
            pl.BlockSpec((tm, 2 * V7X_LANES), lambda i: (i, 0)),
            pl.BlockSpec((tm, 2 * V7X_LANES), lambda i: (i, 0)),
            pl.BlockSpec((V7X_LANES, tm), lambda i: (0, i)),
        ],
        out_shape=[
            jax.ShapeDtypeStruct((n_rows, d_inner), BF16),
            jax.ShapeDtypeStruct((n_rows, conv_dim), BF16),
            jax.ShapeDtypeStruct((n_rows, 2 * V7X_LANES), BF16),
            jax.ShapeDtypeStruct((n_rows, 2 * V7X_LANES), BF16),
            jax.ShapeDtypeStruct((V7X_LANES, n_rows), F32),
        ],
        scratch_shapes=[
            pltpu.VMEM((r, d), BF16),
            pltpu.VMEM((d // V7X_LANES, r, V7X_LANES), F32),
            pltpu.VMEM((tm, d), BF16),
            pltpu.VMEM((PROJ_AHEAD + 1, r, cw), F32),
            pltpu.VMEM((2, cw // V7X_LANES, r, V7X_LANES), F32),
        ],
        compiler_params=_params("parallel"),
        name="ssd_in",
    )(h, h, h, w_in, conv_w, conv_b.reshape(1, conv_dim), w_dt, dt_bias, a_log, selc)


def _ssd_scan_kernel(x_ref, b_ref, c_ref, z_ref, a3_ref, b2_ref, pt_ref, dsk_ref, ng_ref, o_ref,
                     y_ref, sb_ref, xbd_all_ref, sel_l_ref, sel_e_ref, sel_b_ref, *, n_chunks, heads, unroll):
    q = CHUNK
    p = HEAD_DIM
    w = heads * p
    g = pl.program_id(1)
    log_q, log_p = q.bit_length() - 1, p.bit_length() - 1

    src = lax.broadcasted_iota(jnp.int32, sel_l_ref.shape, 0)
    blk = lax.broadcasted_iota(jnp.int32, sel_l_ref.shape, 1) >> log_q
    hd = 4 * (blk & 1) + (blk >> 1)
    sel_l_ref[...] = (((src >> 5) == g) & ((src & 7) == hd) & (((src >> 3) & 3) != 3)).astype(BF16)

    def selection(ref, source_offset):
        src = lax.broadcasted_iota(jnp.int32, ref.shape, 0) & (V7X_LANES - 1)
        out = lax.broadcasted_iota(jnp.int32, ref.shape, 1)
        ref[...] = (src == PACK * g + source_offset(out)).astype(BF16)

    selection(sel_e_ref, lambda j: jnp.where(j < w, j >> log_p, 8 + ((j - w) >> log_p)))
    selection(sel_b_ref, lambda j: 4 + (j >> log_p))

    @pl.when((pl.program_id(0) == 0) & (g == 0))
    def _():
        xbd_all_ref[...] = jnp.zeros_like(xbd_all_ref)

    row = lax.broadcasted_iota(jnp.int32, (q, q), 0)
    col = lax.broadcasted_iota(jnp.int32, (q, q), 1)
    lower = row >= col
    upper = row <= col
    neg_inf = jnp.float32(-jnp.inf)

    def pass1(i, hf):
        chunks = range(unroll[0])
        rows = [pl.ds(pl.multiple_of((i * unroll[0] + u) * q, q), q) for u in chunks]
        xc = [x_ref[rw, :] for rw in rows]
        bc = [b_ref[rw, :] for rw in rows]
        cc = [c_ref[rw, :] for rw in rows]
        pt = [pt_ref[:, rw] for rw in rows]
        trip = pl.ds(pl.multiple_of(i * unroll[0] * q, q), unroll[0] * q)
        a3_trip = a3_ref[trip, :]
        b2_trip = b2_ref[trip, :]
        cbm = [lax.dot_general(cc[u], bc[u], (((1,), (1,)), ((), ())), preferred_element_type=F32)
               for u in chunks]
        ms = [[] for _ in chunks]
        for r in range(heads):
            cum_trip = _dot(a3_trip, sel_l_ref[:, 2 * r * q:2 * (r + 1) * q])
            for u in chunks:
                cum_col = cum_trip[u * q:(u + 1) * q]
                seg_f = cum_col[:, :q] - pt[u][8 + r:9 + r, :]
                seg_b = cum_col[:, q:] - pt[u][12 + r:13 + r, :]
                lf = jnp.exp(jnp.where(lower, seg_f, neg_inf)) * pt[u][r:r + 1, :]
                lb = jnp.exp(jnp.where(upper, seg_b, neg_inf)) * pt[u][4 + r:5 + r, :]
                ms[u].append((cbm[u] * (lf + lb)).astype(BF16))
                xbd_all_ref[u, r * q:(r + 1) * q, r * p:(r + 1) * p] = xc[u][:, r * p:(r + 1) * p]
        yd = [_dot(jnp.concatenate(ms[u], axis=1), xbd_all_ref[u]) for u in chunks]
        xf = [xc[u].astype(F32) for u in chunks]
        ex_trip = _dot(b2_trip, sel_e_ref[...])
        ex = [ex_trip[u * q:(u + 1) * q] for u in chunks]
        xdec = [jnp.concatenate([(xf[u] * ex[u][:, w:2 * w]).astype(BF16),
                                 (xf[u] * ex[u][:, 2 * w:]).astype(BF16)], axis=1)
                for u in chunks]
        st = [lax.dot_general(bc[u], xdec[u], (((0,), (0,)), ((), ())), preferred_element_type=F32)
              for u in chunks]
        ef = [ex[u][:, :w] for u in chunks]
        for u in chunks:
            y_ref[rows[u], :] = yd[u] + _dot(cc[u], hf.astype(BF16)) * ef[u] + dsk_ref[...] * xf[u]
            sb_ref[i * unroll[0] + u] = st[u][:, w:]
            hf = hf * ef[u][q - 1:q, :] + st[u][:, :w]
        return hf

    def pass2(k, hb):
        c = n_chunks - 1 - k
        rows = pl.ds(pl.multiple_of(c * q, q), q)
        eb = _dot(b2_ref[rows, :], sel_b_ref[...])
        y = y_ref[rows, :] + _dot(c_ref[rows, :], hb.astype(BF16)) * eb
        y = y * _silu(z_ref[rows, :].astype(F32))
        y = y * lax.rsqrt(jnp.mean(y * y, axis=-1, keepdims=True) + LN_EPS)
        o_ref[rows, :] = (y * ng_ref[...]).astype(BF16)
        return hb * eb[0:1, :] + sb_ref[c]

    zero = jnp.zeros((D_STATE, w), F32)
    lax.fori_loop(0, n_chunks // unroll[0], pass1, zero)
    lax.fori_loop(0, n_chunks, pass2, zero, unroll=unroll[1])


def _ssd_scan(z, xbc, a3, b2, pack_t, d_skip, norm_g, bsz, seq_len):
    d_inner = z.shape[1]
    groups = SSM_GROUPS
    w = d_inner // groups
    heads = w // HEAD_DIM
    n = D_STATE
    nc = seq_len // CHUNK
    assert nc % SCAN_UNROLL[0] == 0
    z3 = z.reshape(bsz, seq_len, d_inner)
    xbc3 = xbc.reshape(bsz, seq_len, xbc.shape[1])
    a33 = a3.reshape(bsz, seq_len, a3.shape[1])
    b23 = b2.reshape(bsz, seq_len, b2.shape[1])
    b_off = d_inner // n
    c_off = b_off + groups
    return pl.pallas_call(
        functools.partial(_ssd_scan_kernel, n_chunks=nc, heads=heads, unroll=SCAN_UNROLL),
        grid=(bsz, groups),
        in_specs=[
            pl.BlockSpec((None, seq_len, w), lambda b, g: (b, 0, g)),
            pl.BlockSpec((None, seq_len, n), lambda b, g: (b, 0, b_off + g)),
            pl.BlockSpec((None, seq_len, n), lambda b, g: (b, 0, c_off + g)),
            pl.BlockSpec((None, seq_len, w), lambda b, g: (b, 0, g)),
            pl.BlockSpec((None, seq_len, a3.shape[1]), lambda b, g: (b, 0, 0)),
            pl.BlockSpec((None, seq_len, b2.shape[1]), lambda b, g: (b, 0, 0)),
            pl.BlockSpec((PACK, seq_len), lambda b, g: (g, b)),
            pl.BlockSpec((1, w), lambda b, g: (0, g)),
            pl.BlockSpec((1, w), lambda b, g: (0, g)),
        ],
        out_specs=pl.BlockSpec((None, seq_len, w), lambda b, g: (b, 0, g)),
        out_shape=jax.ShapeDtypeStruct((bsz, seq_len, d_inner), BF16),
        scratch_shapes=[
            pltpu.VMEM((seq_len, w), F32),
            pltpu.VMEM((nc, n, w), F32),
            pltpu.VMEM((SCAN_UNROLL[0], heads * CHUNK, w), BF16),
            pltpu.VMEM((a3.shape[1], 2 * heads * CHUNK), BF16),
            pltpu.VMEM((b2.shape[1], 3 * w), BF16),
            pltpu.VMEM((b2.shape[1], w), BF16),
        ],
        compiler_params=_params("arbitrary", "arbitrary"),
        name="ssd_scan",
    )(xbc3, xbc3, xbc3, z3, a33, b23, pack_t, d_skip, norm_g.reshape(1, d_inner))


def _proj_ln_kernel(y_ref, h_ref, w_ref, g_ref, b_ref, o_ref):
    mix = _dot(y_ref[...], w_ref[...])
    o_ref[...] = _layer_norm(DEEPNORM_ALPHA * h_ref[...] + mix, g_ref[...], b_ref[...])


def _proj_ln(y, h, w, ln_g, ln_b):
    n_rows, d = h.shape
    k = y.shape[1]
    tm = min(PLAIN_TILE, n_rows)
    return pl.pallas_call(
        _proj_ln_kernel,
        grid=(n_rows // tm,),
        in_specs=[
            pl.BlockSpec((tm, k), lambda i: (i, 0)),
            pl.BlockSpec((tm, d), lambda i: (i, 0)),
            _const_spec(w.shape), _const_spec((1, d)), _const_spec((1, d)),
        ],
        out_specs=pl.BlockSpec((tm, d), lambda i: (i, 0)),
        out_shape=jax.ShapeDtypeStruct((n_rows, d), F32),
        compiler_params=_params("parallel"),
        name="proj_ln",
    )(y, h, w.astype(BF16), ln_g.reshape(1, d), ln_b.reshape(1, d))


def _channel_dft_table(gd):
    k = np.arange(gd)
    ang = 2.0 * np.pi * ((k[:, None] * k[None, :]) % gd) / gd
    t = np.concatenate([np.cos(ang), np.sin(ang)], axis=1) / math.sqrt(gd)
    return jnp.asarray(t, dtype=BF16)


def _seq_dft_table(s):
    tr = min(DFT_ROW_TILE, s // 2)
    n_half = s // 2 // tr
    n = jnp.arange(s // 2, dtype=jnp.int32)
    row = jnp.arange(tr + HALO, dtype=jnp.int32)
    tile = jnp.arange(n_half, dtype=jnp.int32) * tr
    unit = 2.0 * math.pi / s
    ang_r = unit * ((row[:, None] * n[None, :]) % s).astype(F32)
    ang_t = unit * ((tile[:, None] * n[None, :]) % s).astype(F32)
    scale = 1.0 / math.sqrt(s)
    cr, sr = jnp.cos(ang_r)[None] * scale, jnp.sin(ang_r)[None] * scale
    ct, st = jnp.cos(ang_t)[:, None], jnp.sin(ang_t)[:, None]
    return jnp.concatenate([ct * cr - st * sr, st * cr + ct * sr], axis=2).astype(BF16)


def _cumsum_compaction_matrix():
    m = np.zeros((3 * V7X_LANES, 2 * V7X_LANES), np.float32)
    for g in range(SSM_GROUPS):
        for t in range(3):
            for hd in range(8):
                m[t * V7X_LANES + PACK * g + 8 + hd, 32 * g + 8 * t + hd] = 1.0
    return m


def _packed_lane_heads():
    lane = np.arange(V7X_LANES)
    g, d, r = lane // PACK, (lane // 4) % 2, lane % 4
    return d, g * 4 + r


def kernel(x, emb_ln_g, emb_ln_b, fn_w_in, fn_b_in, fn_w_out, fn_b_out, ssd_w_in, ssd_conv_w, ssd_conv_b, ssd_a_log_fwd, ssd_a_log_bwd, ssd_dt_bias_fwd, ssd_dt_bias_bwd, ssd_d, ssd_norm_g, ssd_w_out, ln_tok_g, ln_tok_b, ff_w_up, ff_conv_w, ff_conv_b, ff_w_down, ln_ffn_g, ln_ffn_b):
    bsz, s, d = x.shape
    n_rows = bsz * s
    d_inner = SSM_EXPAND * d
    n_heads = d_inner // HEAD_DIM
    conv_dim = d_inner + 2 * SSM_GROUPS * D_STATE
    assert s % TOKEN_TILE == 0 and s % CHUNK == 0 and d % (FN_GROUPS * V7X_LANES) == 0
    assert n_heads // SSM_GROUPS == 4 and SSM_GROUPS * PACK == V7X_LANES

    ff_w_up_b, ff_w_down_b = ff_w_up.astype(BF16), ff_w_down.astype(BF16)
    h = None
    for i in range(DEPTH):
        j = i // N_MIXERS
        if i % N_MIXERS == 0:
            cs = _channel_dft_table(d // FN_GROUPS)
            table = _seq_dft_table(s)
            if h is None:
                h0, z = _fnet_in(x, emb_ln_g, emb_ln_b, fn_w_in[j].astype(BF16), fn_b_in[j], cs)
            else:
                raise NotImplementedError("only the first layer uses the Fourier mixer at this depth")
            nyq = z[:, 0, s // 2:s // 2 + 1, :]
            h = _fnet_seq(table, _fnet_fold(z), nyq, h0, fn_w_out[j].astype(BF16), fn_b_out[j],
                          ln_tok_g[i], ln_tok_b[i]).reshape(n_rows, d)
        else:
            w_in = ssd_w_in[j].astype(BF16)
            lane_dir, lane_head = _packed_lane_heads()
            w_dt = ssd_w_in[j][:, d_inner + conv_dim:][:, lane_dir * n_heads + lane_head].astype(BF16)
            pick = lambda f, b: jnp.where(jnp.asarray(lane_dir == 1), b[lane_head], f[lane_head]).reshape(1, V7X_LANES)
            z, xbc, a3, b2, pack_t = _ssd_in(
                h, s, w_in, d_inner, ssd_conv_w[j], ssd_conv_b[j],
                w_dt, pick(ssd_dt_bias_fwd[j], ssd_dt_bias_bwd[j]), pick(ssd_a_log_fwd[j], ssd_a_log_bwd[j]))
            d_skip = jnp.repeat(ssd_d[j], HEAD_DIM).reshape(1, d_inner)
            yn = _ssd_scan(z, xbc, a3, b2, pack_t, d_skip, ssd_norm_g[j], bsz, s)
            h = _proj_ln(yn.reshape(n_rows, d_inner), h, ssd_w_out[j], ln_tok_g[i], ln_tok_b[i])
        h = _conv_ffn(h, s, i, ff_w_up_b, ff_conv_w, ff_conv_b, ff_w_down_b, ln_ffn_g[i], ln_ffn_b[i])
    return h.reshape(bsz, s, d)
```

```python
import functools
import math

import numpy as np
import jax
import jax.numpy as jnp
from jax import lax
from jax.experimental import pallas as pl
from jax.experimental.pallas import tpu as pltpu

F32 = jnp.float32
BF16 = jnp.bfloat16

LN_EPS = 1e-5
DEPTH = 2
N_MIXERS = 2
DEEPNORM_ALPHA = (2.0 * DEPTH) ** 0.25
FN_GROUPS = 8
HEAD_DIM = 64
D_STATE = 128
SSM_GROUPS = 8
SSM_EXPAND = 2
CHUNK = 128

V7X_LANES = 128
V7X_BF16_SUBLANES = 16
V7X_VMEM_LIMIT_BYTES = 56 * 1024 * 1024

TOKEN_TILE = 512
PLAIN_TILE = 1024
HALO = V7X_BF16_SUBLANES
FF_CHUNK = 256
XBC_CHUNK = 256
DFT_ROW_TILE = 512
PACK = 16
PROJ_AHEAD = 2
SCAN_UNROLL = (16, 8)


def _layer_norm(v, g, b):
    mu = jnp.mean(v, axis=-1, keepdims=True)
    vc = v - mu
    var = jnp.mean(vc * vc, axis=-1, keepdims=True)
    return vc * lax.rsqrt(var + LN_EPS) * g + b


def _silu(v):
    hv = 0.5 * v
    return hv + hv * jnp.tanh(hv)


def _dot(a, b):
    return jnp.dot(a, b, preferred_element_type=F32)


def _split_bf16(v, terms):
    out = []
    for _ in range(terms):
        t = v.astype(BF16)
        out.append(t)
        v = v - t.astype(F32)
    return out


def _params(*semantics):
    return pltpu.CompilerParams(dimension_semantics=semantics,
                                vmem_limit_bytes=V7X_VMEM_LIMIT_BYTES)


def _const_spec(shape):
    zeros = (0,) * len(shape)
    return pl.BlockSpec(shape, lambda *_: zeros, pipeline_mode=pl.Buffered(1))


def _layer_spec(stacked, layer):
    zeros = (0,) * (stacked.ndim - 1)
    return pl.BlockSpec((None,) + stacked.shape[1:], lambda *_: (layer,) + zeros, pipeline_mode=pl.Buffered(1))


def _fnet_in_kernel(x_ref, g_ref, b_ref, w_ref, bi_ref, cs_ref, h_ref, z_ref, *, groups):
    h = _layer_norm(x_ref[...], g_ref[...], b_ref[...])
    h_ref[...] = h
    u = (_dot(h.astype(BF16), w_ref[...]) + bi_ref[...]).astype(BF16)
    gd = u.shape[1] // groups
    for g in range(groups):
        r = _dot(u[:, g * gd:(g + 1) * gd], cs_ref[...])
        z_ref[0, :, g * gd:(g + 1) * gd] = r[:, :gd].astype(BF16)
        z_ref[1, :, g * gd:(g + 1) * gd] = r[:, gd:].astype(BF16)


def _fnet_in(x, ln_g, ln_b, w_in, b_in, cs):
    bsz, s, d = x.shape
    tm = min(PLAIN_TILE, s)
    nt = s // tm
    return pl.pallas_call(
        functools.partial(_fnet_in_kernel, groups=FN_GROUPS),
        grid=(bsz, nt),
        in_specs=[
            pl.BlockSpec((None, tm, d), lambda b, i: (b, i, 0)),
            _const_spec((1, d)), _const_spec((1, d)),
            _const_spec(w_in.shape), _const_spec((1, d)), _const_spec(cs.shape),
        ],
        out_specs=[
            pl.BlockSpec((None, tm, d), lambda b, i: (b, i, 0)),
            pl.BlockSpec((None, 2, tm, d), lambda b, i: (b, 0, i, 0)),
        ],
        out_shape=[jax.ShapeDtypeStruct((bsz, s, d), F32),
                   jax.ShapeDtypeStruct((bsz, 2, s, d), BF16)],
        compiler_params=_params("parallel", "parallel"),
        name="fnet_in",
    )(x, ln_g.reshape(1, d), ln_b.reshape(1, d), w_in, b_in.reshape(1, d), cs)


def _fnet_fold_kernel(zt_ref, zm_ref, ze_ref, p_ref, o_ref):
    tf = zt_ref.shape[1]
    keep = (pl.program_id(1) != 0).astype(F32)
    first_row = lax.broadcasted_iota(jnp.int32, (tf, 1), 0) == 0
    for part, sign in ((0, 1.0), (1, -1.0)):
        mirrored = jnp.where(first_row, ze_ref[part, 0:1, :].astype(F32) * keep, _dot(p_ref[...], zm_ref[part]))
        o_ref[part] = (zt_ref[part].astype(F32) + sign * mirrored).astype(BF16)


def _fnet_fold(z):
    bsz, _, s, d = z.shape
    tf = min(DFT_ROW_TILE, s // 2)
    nt = s // tf
    flip = np.zeros((tf, tf), np.float32)
    flip[np.arange(1, tf), tf - np.arange(1, tf)] = 1.0
    per = tf // HALO
    last = s // HALO - 1
    return pl.pallas_call(
        _fnet_fold_kernel,
        grid=(bsz, nt // 2),
        in_specs=[
            pl.BlockSpec((None, 2, tf, d), lambda b, j: (b, 0, j, 0)),
            pl.BlockSpec((None, 2, tf, d), lambda b, j: (b, 0, nt - 1 - j, 0)),
            pl.BlockSpec((None, 2, HALO, d), lambda b, j: (b, 0, jnp.minimum((nt - j) * per, last), 0)),
            _const_spec(flip.shape),
        ],
        out_specs=pl.BlockSpec((None, 2, tf, d), lambda b, j: (b, 0, j, 0)),
        out_shape=jax.ShapeDtypeStruct((bsz, 2, s // 2, d), BF16),
        compiler_params=_params("parallel", "parallel"),
        name="fnet_fold",
    )(z, z, z, jnp.asarray(flip, dtype=BF16))


def _fnet_seq_kernel(t_ref, z_ref, nyq_ref, h_ref, p_ref, w_ref, bo_ref, g_ref, b_ref, o_ref, y_ref, stash_ref,
                     *, n_half):
    i = pl.program_id(1)
    tr = y_ref.shape[0]
    sh = z_ref.shape[0] // 2

    @pl.when(i < n_half)
    def _():
        rows = t_ref.shape[0]
        odd = (lax.broadcasted_iota(jnp.int32, (rows, 1), 0) & 1) == 1
        nyq = nyq_ref[...].astype(F32) * (1.0 / math.sqrt(2 * sh))
        a = _dot(t_ref[:, :sh], z_ref[:sh, :]) + jnp.where(odd, -nyq, nyq)
        b = _dot(t_ref[:, sh:], z_ref[sh:, :])
        y_ref[...] = (a - b)[:tr].astype(BF16)
        stash_ref[i] = (a + b).astype(BF16)

    @pl.when(i >= n_half)
    def _():
        t = 2 * n_half - 1 - i
        first_row = lax.broadcasted_iota(jnp.int32, (tr, 1), 0) == 0
        flipped = _dot(p_ref[...], stash_ref[t, 0:tr, :])
        y_ref[...] = jnp.where(first_row, stash_ref[t, tr:tr + 1, :].astype(F32), flipped).astype(BF16)

    mix = _dot(y_ref[...], w_ref[...]) + bo_ref[...]
    o_ref[...] = _layer_norm(DEEPNORM_ALPHA * h_ref[...] + mix, g_ref[...], b_ref[...])


def _fnet_seq(table, zf, nyq, h, w_out, b_out, ln_g, ln_b):
    bsz, s, d = h.shape
    n_half, rows, _ = table.shape
    tr = rows - HALO
    assert tr % 2 == 0
    z2 = zf.reshape(bsz, s, d)
    flip = np.zeros((tr, tr), np.float32)
    flip[np.arange(1, tr), tr - np.arange(1, tr)] = 1.0
    return pl.pallas_call(
        functools.partial(_fnet_seq_kernel, n_half=n_half),
        grid=(bsz, 2 * n_half),
        in_specs=[
            pl.BlockSpec((None, rows, s), lambda b, i: (jnp.minimum(i, n_half - 1), 0, 0)),
            pl.BlockSpec((None, s, d), lambda b, i: (b, 0, 0)),
            pl.BlockSpec((None, 1, d), lambda b, i: (b, 0, 0)),
            pl.BlockSpec((None, tr, d), lambda b, i: (b, i, 0)),
            _const_spec(flip.shape), _const_spec(w_out.shape),
            _const_spec((1, d)), _const_spec((1, d)), _const_spec((1, d)),
        ],
        out_specs=pl.BlockSpec((None, tr, d), lambda b, i: (b, i, 0)),
        out_shape=jax.ShapeDtypeStruct((bsz, s, d), F32),
        scratch_shapes=[
            pltpu.VMEM((tr, d), BF16),
            pltpu.VMEM((n_half, rows, d), BF16),
        ],
        compiler_params=_params("parallel", "arbitrary"),
        name="fnet_seq",
    )(table, z2, nyq, h, jnp.asarray(flip, dtype=BF16), w_out, b_out.reshape(1, d),
      ln_g.reshape(1, d), ln_b.reshape(1, d))


def _halo_specs(tm, d, n_rows):
    per = tm // HALO
    last = n_rows // HALO - 1
    return [
        pl.BlockSpec((HALO, d), lambda i: (jnp.maximum(i * per - 1, 0), 0)),
        pl.BlockSpec((tm, d), lambda i: (i, 0)),
        pl.BlockSpec((HALO, d), lambda i: (jnp.minimum((i + 1) * per, last), 0)),
    ]


def _assemble_interleaved(xp_ref, nat_ref, hp_ref, hm_ref, hn_ref, tiles_per_seq):
    tm, d = hm_ref.shape
    r = tm + 2 * HALO
    groups = r // 8
    pos = lax.rem(pl.program_id(0), tiles_per_seq)
    keep_prev = (pos != 0).astype(F32)
    keep_next = (pos != tiles_per_seq - 1).astype(F32)
    lane_tiles = d // V7X_LANES
    for l in range(lane_tiles):
        lanes = slice(l * V7X_LANES, (l + 1) * V7X_LANES)
        nat_ref[l, 0:HALO, :] = hp_ref[:, lanes] * keep_prev
        nat_ref[l, HALO:HALO + tm, :] = hm_ref[:, lanes]
        nat_ref[l, HALO + tm:, :] = hn_ref[:, lanes] * keep_next
    gather = lambda j: jnp.concatenate(
        [nat_ref[l, pl.ds(j, 8, stride=groups), :] for l in range(lane_tiles)], axis=1)
    for j in range(0, groups, 2):
        xp_ref[8 * j:8 * j + 16, :] = jnp.concatenate([gather(j), gather(j + 1)], axis=0).astype(BF16)


def _natural_rows(val, un_ref):
    r, c = val.shape
    groups = r // 8
    lane_tiles = c // V7X_LANES
    for j in range(groups):
        for l in range(lane_tiles):
            un_ref[l, pl.ds(j, 8, stride=groups), :] = val[8 * j:8 * j + 8, l * V7X_LANES:(l + 1) * V7X_LANES]
    return jnp.concatenate([un_ref[l, HALO:r - HALO, :] for l in range(lane_tiles)], axis=1)


def _tap(u, delta):
    r = u.shape[0]
    if delta == 0:
        return u
    if delta > 0:
        wrap = pltpu.roll(u[:8 * delta], 8 * delta - 1, 0)
        return jnp.concatenate([u[8 * delta:], wrap], axis=0)
    wrap = pltpu.roll(u[r + 8 * delta:], 1, 0)
    return jnp.concatenate([wrap, u[:r + 8 * delta]], axis=0)


def _conv_taps(u, w, bias):
    width = w.shape[0]
    half = width // 2
    out = w[half:half + 1, :] * u + bias
    for k in range(width):
        if k != half:
            out = out + w[k:k + 1, :] * _tap(u, k - half)
    return out


def _conv_ffn_kernel(hp_ref, hm_ref, hn_ref, wu_ref, cw_ref, cb_ref, wd_ref, g_ref, b_ref, o_ref,
                     xp_ref, nat_ref, ug_ref, uv_ref, act_ref, *, tiles_per_seq, n_chunks):
    cw = ug_ref.shape[2]
    d_ff = n_chunks * cw
    slots = ug_ref.shape[0]
    _assemble_interleaved(xp_ref, nat_ref, hp_ref, hm_ref, hn_ref, tiles_per_seq)

    def project(c):
        gate, val = slice(c * cw, (c + 1) * cw), slice(d_ff + c * cw, d_ff + (c + 1) * cw)
        ug_ref[c % slots] = _dot(xp_ref[...], wu_ref[:, gate])
        uv_ref[c % slots] = _dot(xp_ref[...], wu_ref[:, val])

    for c in range(min(PROJ_AHEAD, n_chunks)):
        project(c)
    for c in range(n_chunks):
        if c + PROJ_AHEAD < n_chunks:
            project(c + PROJ_AHEAD)
        gate, val = slice(c * cw, (c + 1) * cw), slice(d_ff + c * cw, d_ff + (c + 1) * cw)
        cg = _conv_taps(ug_ref[c % slots], cw_ref[:, gate], cb_ref[:, gate])
        cv = _conv_taps(uv_ref[c % slots], cw_ref[:, val], cb_ref[:, val])
        act_ref[:, gate] = (_silu(cg) * cv).astype(BF16)
    ffn = _natural_rows(_dot(act_ref[...], wd_ref[...]), nat_ref)
    o_ref[...] = _layer_norm(DEEPNORM_ALPHA * hm_ref[...] + ffn, g_ref[...], b_ref[...])


def _conv_ffn(h, seq_len, layer, w_up, conv_w, conv_b, w_down, ln_g, ln_b):
    n_rows, d = h.shape
    d_ff = w_down.shape[1]
    tm = min(TOKEN_TILE, seq_len)
    r = tm + 2 * HALO
    cw = FF_CHUNK
    nc = d_ff // cw
    assert d_ff % cw == 0 and conv_w.shape[1] // 2 <= HALO
    conv_b3 = conv_b.reshape(conv_b.shape[0], 1, 2 * d_ff)
    return pl.pallas_call(
        functools.partial(_conv_ffn_kernel, tiles_per_seq=seq_len // tm, n_chunks=nc),
        grid=(n_rows // tm,),
        in_specs=_halo_specs(tm, d, n_rows) + [
            _layer_spec(w_up, layer), _layer_spec(conv_w, layer), _layer_spec(conv_b3, layer),
            _layer_spec(w_down, layer), _const_spec((1, d)), _const_spec((1, d)),
        ],
        out_specs=pl.BlockSpec((tm, d), lambda i: (i, 0)),
        out_shape=jax.ShapeDtypeStruct((n_rows, d), F32),
        scratch_shapes=[
            pltpu.VMEM((r, d), BF16),
            pltpu.VMEM((d // V7X_LANES, r, V7X_LANES), F32),
            pltpu.VMEM((PROJ_AHEAD + 1, r, cw), F32),
            pltpu.VMEM((PROJ_AHEAD + 1, r, cw), F32),
            pltpu.VMEM((r, d_ff), BF16),
        ],
        compiler_params=_params("parallel"),
        name="conv_ffn",
    )(h, h, h, w_up, conv_w, conv_b3, w_down, ln_g.reshape(1, d), ln_b.reshape(1, d))


def _ssd_in_kernel(hp_ref, hm_ref, hn_ref, win_ref, cw_ref, cb_ref, wdt_ref, dtb_ref, alog_ref,
                   selc_ref, z_ref, xbc_ref, a3_ref, b2_ref, pt_ref, xp_ref, nat_ref, xn_ref, u_ref, un_ref,
                   *, tiles_per_seq, n_chunks):
    tm = hm_ref.shape[0]
    cw = u_ref.shape[2]
    slots = u_ref.shape[0]
    d_inner = z_ref.shape[1]
    _assemble_interleaved(xp_ref, nat_ref, hp_ref, hm_ref, hn_ref, tiles_per_seq)
    xn_ref[...] = hm_ref[...].astype(BF16)

    lane = lax.broadcasted_iota(jnp.int32, (1, V7X_LANES), 1)
    is_cum = (lane & 8) != 0
    is_bwd = (lane & 4) != 0
    row = lax.broadcasted_iota(jnp.int32, (CHUNK, CHUNK), 0)
    col = lax.broadcasted_iota(jnp.int32, (CHUNK, CHUNK), 1)
    tri = (row >= col).astype(BF16)
    n_sub = tm // CHUNK
    state = {}

    def dt_project():
        raw = _dot(xn_ref[...], wdt_ref[...]) + dtb_ref[...]
        state["dt"] = jnp.maximum(raw, 0.0) + jnp.log(1.0 + jnp.exp(-jnp.abs(raw)))
        state["dta"] = state["dt"] * -jnp.exp(alog_ref[...])

    def dt_cumsum(k):
        dk = state["dta"][k * CHUNK:(k + 1) * CHUNK, :]
        t0, t1, t2 = _split_bf16(dk, 3)
        state["cf", k] = _dot(tri, t2) + _dot(tri, t1) + _dot(tri, t0)

    def dt_pack(k):
        rows = slice(k * CHUNK, (k + 1) * CHUNK)
        dk, cf, dtk = state["dta"][rows, :], state["cf", k], state["dt"][rows, :]
        cb = cf[CHUNK - 1:CHUNK, :] - cf + dk
        cum = jnp.where(is_bwd, cb, cf)
        end = jnp.where(is_bwd, cum[0:1, :], cum[CHUNK - 1:CHUNK, :])
        pa = jnp.where(is_cum, cum, dtk)
        pb = jnp.where(is_cum, dtk * jnp.exp(end - cum), jnp.exp(cum))
        a3_ref[rows, :] = _dot(jnp.concatenate(_split_bf16(pa, 3), axis=1), selc_ref[...]).astype(BF16)
        for j, term in enumerate(_split_bf16(pb, 2)):
            b2_ref[rows, j * V7X_LANES:(j + 1) * V7X_LANES] = term
        pt_ref[:, rows] = pa.T

    stages = ([dt_project] + [functools.partial(dt_cumsum, k) for k in range(n_sub)]
              + [functools.partial(dt_pack, k) for k in range(n_sub)])

    def project(c):
        u_ref[c % slots] = _dot(xp_ref[...], win_ref[:, d_inner + c * cw:d_inner + (c + 1) * cw])

    z_steps = d_inner // cw
    for c in range(min(PROJ_AHEAD, n_chunks)):
        project(c)
    for c in range(n_chunks):
        if c + PROJ_AHEAD < n_chunks:
            project(c + PROJ_AHEAD)
        if c < len(stages):
            stages[c]()
        for s in range(c * z_steps // n_chunks, (c + 1) * z_steps // n_chunks):
            zc = slice(s * cw, (s + 1) * cw)
            z_ref[:, zc] = _dot(xn_ref[...], win_ref[:, zc]).astype(BF16)
        cols = slice(c * cw, (c + 1) * cw)
        v = _silu(_conv_taps(u_ref[c % slots], cw_ref[:, cols], cb_ref[:, cols]))
        xbc_ref[:, cols] = _natural_rows(v, un_ref.at[c % 2]).astype(BF16)
    for stage in stages[n_chunks:]:
        stage()


def _ssd_in(h, seq_len, w_in, d_inner, conv_w, conv_b, w_dt, dt_bias, a_log):
    n_rows, d = h.shape
    width, conv_dim = conv_w.shape
    tm = min(TOKEN_TILE, seq_len)
    r = tm + 2 * HALO
    cw = XBC_CHUNK
    nc = conv_dim // cw
    assert conv_dim % cw == 0 and d_inner % V7X_LANES == 0 and width // 2 <= HALO
    selc = jnp.asarray(_cumsum_compaction_matrix(), dtype=BF16)
    return pl.pallas_call(
        functools.partial(_ssd_in_kernel, tiles_per_seq=seq_len // tm, n_chunks=nc),
        grid=(n_rows // tm,),
        in_specs=_halo_specs(tm, d, n_rows) + [
            _const_spec(w_in.shape), _const_spec(conv_w.shape), _const_spec((1, conv_dim)),
            _const_spec(w_dt.shape), _const_spec((1, V7X_LANES)), _const_spec((1, V7X_LANES)),
            _const_spec(selc.shape),
        ],
        out_specs=[
            pl.BlockSpec((tm, d_inner), lambda i: (i, 0)),
            pl.BlockSpec((tm, conv_dim), lambda i: (i, 0)),
            pl.BlockSpec((tm, 2 * V7X_LANES), lambda i: (i, 0)),
            pl.BlockSpec((tm, 2 * V7X_LANES), lambda i: (i, 0)),
            pl.BlockSpec((V7X_LANES, tm), lambda i: (0, i)),
        ],
        out_shape=[
            jax.ShapeDtypeStruct((n_rows, d_inner), BF16),
            jax.ShapeDtypeStruct((n_rows, conv_dim), BF16),
            jax.ShapeDtypeStruct((n_rows, 2 * V7X_LANES), BF16),
            jax.ShapeDtypeStruct((n_rows, 2 * V7X_LANES), BF16),
            jax.ShapeDtypeStruct((V7X_LANES, n_rows), F32),
        ],
        scratch_shapes=[
            pltpu.VMEM((r, d), BF16),
            pltpu.VMEM((d // V7X_LANES, r, V7X_LANES), F32),
            pltpu.VMEM((tm, d), BF16),
            pltpu.VMEM((PROJ_AHEAD + 1, r, cw), F32),
            pltpu.VMEM((2, cw // V7X_LANES, r, V7X_LANES), F32),
        ],
        compiler_params=_params("parallel"),
        name="ssd_in",
    )(h, h, h, w_in, conv_w, conv_b.reshape(1, conv_dim), w_dt, dt_bias, a_log, selc)


def _ssd_scan_kernel(x_ref, b_ref, c_ref, z_ref, a3_ref, b2_ref, pt_ref, dsk_ref, ng_ref, o_ref,
                     y_ref, sb_ref, xbd_all_ref, sel_l_ref, sel_e_ref, sel_b_ref, *, n_chunks, heads, unroll):
    q = CHUNK
    p = HEAD_DIM
    w = heads * p
    g = pl.program_id(1)
    log_q, log_p = q.bit_length() - 1, p.bit_length() - 1

    src = lax.broadcasted_iota(jnp.int32, sel_l_ref.shape, 0)
    blk = lax.broadcasted_iota(jnp.int32, sel_l_ref.shape, 1) >> log_q
    hd = 4 * (blk & 1) + (blk >> 1)
    sel_l_ref[...] = (((src >> 5) == g) & ((src & 7) == hd) & (((src >> 3) & 3) != 3)).astype(BF16)

    def selection(ref, source_offset):
        src = lax.broadcasted_iota(jnp.int32, ref.shape, 0) & (V7X_LANES - 1)
        out = lax.broadcasted_iota(jnp.int32, ref.shape, 1)
        ref[...] = (src == PACK * g + source_offset(out)).astype(BF16)

    selection(sel_e_ref, lambda j: jnp.where(j < w, j >> log_p, 8 + ((j - w) >> log_p)))
    selection(sel_b_ref, lambda j: 4 + (j >> log_p))

    @pl.when((pl.program_id(0) == 0) & (g == 0))
    def _():
        xbd_all_ref[...] = jnp.zeros_like(xbd_all_ref)

    row = lax.broadcasted_iota(jnp.int32, (q, q), 0)
    col = lax.broadcasted_iota(jnp.int32, (q, q), 1)
    lower = row >= col
    upper = row <= col
    neg_inf = jnp.float32(-jnp.inf)

    def pass1(i, hf):
        chunks = range(unroll[0])
        rows = [pl.ds(pl.multiple_of((i * unroll[0] + u) * q, q), q) for u in chunks]
        xc = [x_ref[rw, :] for rw in rows]
        bc = [b_ref[rw, :] for rw in rows]
        cc = [c_ref[rw, :] for rw in rows]
        pt = [pt_ref[:, rw] for rw in rows]
        trip = pl.ds(pl.multiple_of(i * unroll[0] * q, q), unroll[0] * q)
        a3_trip = a3_ref[trip, :]
        b2_trip = b2_ref[trip, :]
        cbm = [lax.dot_general(cc[u], bc[u], (((1,), (1,)), ((), ())), preferred_element_type=F32)
               for u in chunks]
        ms = [[] for _ in chunks]
        for r in range(heads):
            cum_trip = _dot(a3_trip, sel_l_ref[:, 2 * r * q:2 * (r + 1) * q])
            for u in chunks:
                cum_col = cum_trip[u * q:(u + 1) * q]
                seg_f = cum_col[:, :q] - pt[u][8 + r:9 + r, :]
                seg_b = cum_col[:, q:] - pt[u][12 + r:13 + r, :]
                lf = jnp.exp(jnp.where(lower, seg_f, neg_inf)) * pt[u][r:r + 1, :]
                lb = jnp.exp(jnp.where(upper, seg_b, neg_inf)) * pt[u][4 + r:5 + r, :]
                ms[u].append((cbm[u] * (lf + lb)).astype(BF16))
                xbd_all_ref[u, r * q:(r + 1) * q, r * p:(r + 1) * p] = xc[u][:, r * p:(r + 1) * p]
        yd = [_dot(jnp.concatenate(ms[u], axis=1), xbd_all_ref[u]) for u in chunks]
        xf = [xc[u].astype(F32) for u in chunks]
        ex_trip = _dot(b2_trip, sel_e_ref[...])
        ex = [ex_trip[u * q:(u + 1) * q] for u in chunks]
        xdec = [jnp.concatenate([(xf[u] * ex[u][:, w:2 * w]).astype(BF16),
                                 (xf[u] * ex[u][:, 2 * w:]).astype(BF16)], axis=1)
                for u in chunks]
        st = [lax.dot_general(bc[u], xdec[u], (((0,), (0,)), ((), ())), preferred_element_type=F32)
              for u in chunks]
        ef = [ex[u][:, :w] for u in chunks]
        for u in chunks:
            y_ref[rows[u], :] = yd[u] + _dot(cc[u], hf.astype(BF16)) * ef[u] + dsk_ref[...] * xf[u]
            sb_ref[i * unroll[0] + u] = st[u][:, w:]
            hf = hf * ef[u][q - 1:q, :] + st[u][:, :w]
        return hf

    def pass2(k, hb):
        c = n_chunks - 1 - k
        rows = pl.ds(pl.multiple_of(c * q, q), q)
        eb = _dot(b2_ref[rows, :], sel_b_ref[...])
        y = y_ref[rows, :] + _dot(c_ref[rows, :], hb.astype(BF16)) * eb
        y = y * _silu(z_ref[rows, :].astype(F32))
        y = y * lax.rsqrt(jnp.mean(y * y, axis=-1, keepdims=True) + LN_EPS)
        o_ref[rows, :] = (y * ng_ref[...]).astype(BF16)
        return hb * eb[0:1, :] + sb_ref[c]

    zero = jnp.zeros((D_STATE, w), F32)
    lax.fori_loop(0, n_chunks // unroll[0], pass1, zero)
    lax.fori_loop(0, n_chunks, pass2, zero, unroll=unroll[1])


def _ssd_scan(z, xbc, a3, b2, pack_t, d_skip, norm_g, bsz, seq_len):
    d_inner = z.shape[1]
    groups = SSM_GROUPS
    w = d_inner // groups
    heads = w // HEAD_DIM
    n = D_STATE
    nc = seq_len // CHUNK
    assert nc % SCAN_UNROLL[0] == 0
    z3 = z.reshape(bsz, seq_len, d_inner)
    xbc3 = xbc.reshape(bsz, seq_len, xbc.shape[1])
    a33 = a3.reshape(bsz, seq_len, a3.shape[1])
    b23 = b2.reshape(bsz, seq_len, b2.shape[1])
    b_off = d_inner // n
    c_off = b_off + groups
    return pl.pallas_call(
        functools.partial(_ssd_scan_kernel, n_chunks=nc, heads=heads, unroll=SCAN_UNROLL),
        grid=(bsz, groups),
        in_specs=[
            pl.BlockSpec((None, seq_len, w), lambda b, g: (b, 0, g)),
            pl.BlockSpec((None, seq_len, n), lambda b, g: (b, 0, b_off + g)),
            pl.BlockSpec((None, seq_len, n), lambda b, g: (b, 0, c_off + g)),
            pl.BlockSpec((None, seq_len, w), lambda b, g: (b, 0, g)),
            pl.BlockSpec((None, seq_len, a3.shape[1]), lambda b, g: (b, 0, 0)),
            pl.BlockSpec((None, seq_len, b2.shape[1]), lambda b, g: (b, 0, 0)),
            pl.BlockSpec((PACK, seq_len), lambda b, g: (g, b)),
            pl.BlockSpec((1, w), lambda b, g: (0, g)),
            pl.BlockSpec((1, w), lambda b, g: (0, g)),
        ],
        out_specs=pl.BlockSpec((None, seq_len, w), lambda b, g: (b, 0, g)),
        out_shape=jax.ShapeDtypeStruct((bsz, seq_len, d_inner), BF16),
        scratch_shapes=[
            pltpu.VMEM((seq_len, w), F32),
            pltpu.VMEM((nc, n, w), F32),
            pltpu.VMEM((SCAN_UNROLL[0], heads * CHUNK, w), BF16),
            pltpu.VMEM((a3.shape[1], 2 * heads * CHUNK), BF16),
            pltpu.VMEM((b2.shape[1], 3 * w), BF16),
            pltpu.VMEM((b2.shape[1], w), BF16),
        ],
        compiler_params=_params("arbitrary", "arbitrary"),
        name="ssd_scan",
    )(xbc3, xbc3, xbc3, z3, a33, b23, pack_t, d_skip, norm_g.reshape(1, d_inner))


def _proj_ln_kernel(y_ref, h_ref, w_ref, g_ref, b_ref, o_ref):
    mix = _dot(y_ref[...], w_ref[...])
    o_ref[...] = _layer_norm(DEEPNORM_ALPHA * h_ref[...] + mix, g_ref[...], b_ref[...])


def _proj_ln(y, h, w, ln_g, ln_b):
    n_rows, d = h.shape
    k = y.shape[1]
    tm = min(PLAIN_TILE, n_rows)
    return pl.pallas_call(
        _proj_ln_kernel,
        grid=(n_rows // tm,),
        in_specs=[
            pl.BlockSpec((tm, k), lambda i: (i, 0)),
            pl.BlockSpec((tm, d), lambda i: (i, 0)),
            _const_spec(w.shape), _const_spec((1, d)), _const_spec((1, d)),
        ],
        out_specs=pl.BlockSpec((tm, d), lambda i: (i, 0)),
        out_shape=jax.ShapeDtypeStruct((n_rows, d), F32),
        compiler_params=_params("parallel"),
        name="proj_ln",
    )(y, h, w.astype(BF16), ln_g.reshape(1, d), ln_b.reshape(1, d))


def _channel_dft_table(gd):
    k = np.arange(gd)
    ang = 2.0 * np.pi * ((k[:, None] * k[None, :]) % gd) / gd
    t = np.concatenate([np.cos(ang), np.sin(ang)], axis=1) / math.sqrt(gd)
    return jnp.asarray(t, dtype=BF16)


def _seq_dft_table(s):
    tr = min(DFT_ROW_TILE, s // 2)
    n_half = s // 2 // tr
    n = jnp.arange(s // 2, dtype=jnp.int32)
    row = jnp.arange(tr + HALO, dtype=jnp.int32)
    tile = jnp.arange(n_half, dtype=jnp.int32) * tr
    unit = 2.0 * math.pi / s
    ang_r = unit * ((row[:, None] * n[None, :]) % s).astype(F32)
    ang_t = unit * ((tile[:, None] * n[None, :]) % s).astype(F32)
    scale = 1.0 / math.sqrt(s)
    cr, sr = jnp.cos(ang_r)[None] * scale, jnp.sin(ang_r)[None] * scale
    ct, st = jnp.cos(ang_t)[:, None], jnp.sin(ang_t)[:, None]
    return jnp.concatenate([ct * cr - st * sr, st * cr + ct * sr], axis=2).astype(BF16)


def _cumsum_compaction_matrix():
    m = np.zeros((3 * V7X_LANES, 2 * V7X_LANES), np.float32)
    for g in range(SSM_GROUPS):
        for t in range(3):
            for hd in range(8):
                m[t * V7X_LANES + PACK * g + 8 + hd, 32 * g + 8 * t + hd] = 1.0
    return m


def _packed_lane_heads():
    lane = np.arange(V7X_LANES)
    g, d, r = lane // PACK, (lane // 4) % 2, lane % 4
    return d, g * 4 + r


def kernel(x, emb_ln_g, emb_ln_b, fn_w_in, fn_b_in, fn_w_out, fn_b_out, ssd_w_in, ssd_conv_w, ssd_conv_b, ssd_a_log_fwd, ssd_a_log_bwd, ssd_dt_bias_fwd, ssd_dt_bias_bwd, ssd_d, ssd_norm_g, ssd_w_out, ln_tok_g, ln_tok_b, ff_w_up, ff_conv_w, ff_conv_b, ff_w_down, ln_ffn_g, ln_ffn_b):
    bsz, s, d = x.shape
    n_rows = bsz * s
    d_inner = SSM_EXPAND * d
    n_heads = d_inner // HEAD_DIM
    conv_dim = d_inner + 2 * SSM_GROUPS * D_STATE
    assert s % TOKEN_TILE == 0 and s % CHUNK == 0 and d % (FN_GROUPS * V7X_LANES) == 0
    assert n_heads // SSM_GROUPS == 4 and SSM_GROUPS * PACK == V7X_LANES

    ff_w_up_b, ff_w_down_b = ff_w_up.astype(BF16), ff_w_down.astype(BF16)
    h = None
    for i in range(DEPTH):
        j = i // N_MIXERS
        if i % N_MIXERS == 0:
            cs = _channel_dft_table(d // FN_GROUPS)
            table = _seq_dft_table(s)
            if h is None:
                h0, z = _fnet_in(x, emb_ln_g, emb_ln_b, fn_w_in[j].astype(BF16), fn_b_in[j], cs)
            else:
                raise NotImplementedError("only the first layer uses the Fourier mixer at this depth")
            nyq = z[:, 0, s // 2:s // 2 + 1, :]
            h = _fnet_seq(table, _fnet_fold(z), nyq, h0, fn_w_out[j].astype(BF16), fn_b_out[j],
                          ln_tok_g[i], ln_tok_b[i]).reshape(n_rows, d)
        else:
            w_in = ssd_w_in[j].astype(BF16)
            lane_dir, lane_head = _packed_lane_heads()
            w_dt = ssd_w_in[j][:, d_inner + conv_dim:][:, lane_dir * n_heads + lane_head].astype(BF16)
            pick = lambda f, b: jnp.where(jnp.asarray(lane_dir == 1), b[lane_head], f[lane_head]).reshape(1, V7X_LANES)
            z, xbc, a3, b2, pack_t = _ssd_in(
                h, s, w_in, d_inner, ssd_conv_w[j], ssd_conv_b[j],
                w_dt, pick(ssd_dt_bias_fwd[j], ssd_dt_bias_bwd[j]), pick(ssd_a_log_fwd[j], ssd_a_log_bwd[j]))
            d_skip = jnp.repeat(ssd_d[j], HEAD_DIM).reshape(1, d_inner)
            yn = _ssd_scan(z, xbc, a3, b2, pack_t, d_skip, ssd_norm_g[j], bsz, s)
            h = _proj_ln(yn.reshape(n_rows, d_inner), h, ssd_w_out[j], ln_tok_g[i], ln_tok_b[i])
        h = _conv_ffn(h, s, i, ff_w_up_b, ff_conv_w, ff_conv_b, ff_w_down_b, ln_ffn_g[i], ln_ffn_b[i])
    return h.reshape(bsz, s, d)
```
